```python
import math
import jax, jax.numpy as jnp
from jax import lax
import numpy as np

D_MODEL = 1024
BATCH = 1
SEQ = 16384
DEPTH = 4
DEC_BATCH = 8
DEC_SEQ = 64
PAST_LEN = 4096

CHUNK = 64
D_MIX = D_MODEL
D_GMLP = D_MIX // 2
GMLP_BLOCK = 128
GMLP_HEAD = 128
N_GMLP_HEADS = D_GMLP // GMLP_HEAD
D_DIFF = D_MIX - D_GMLP
N_DIFF_HEADS = 4
DIFF_V_DIM = D_DIFF // N_DIFF_HEADS
DIFF_QK_DIM = DIFF_V_DIM // 2
QK_W = N_DIFF_HEADS * 2 * DIFF_QK_DIM
OFF_Q = 2 * D_GMLP
OFF_K = OFF_Q + QK_W
OFF_V = OFF_K + QK_W
D_IN = OFF_V + D_DIFF
N_EXPERTS = 16
N_GROUPS = 4
EXPERTS_PER_GROUP = N_EXPERTS // N_GROUPS
TOP_K = 2
D_EXPERT = 512
Q_BLOCK = 128
ALPHA = (2.0 * DEPTH) ** 0.25
BETA = (8.0 * DEPTH) ** -0.25
LN_EPS = 1e-5
RMS_EPS = 1e-5
ALIBI_SLOPES = tuple(2.0 ** (-8.0 * (h + 1) / N_DIFF_HEADS) for h in range(N_DIFF_HEADS))

kernel_name = 'hymba_gmlp_diffattn_deepnorm_grouped_moe_stream'


def layer_norm(x, g, b):
    xf = x.astype(jnp.float32)
    mu = jnp.mean(xf, -1, keepdims=True)
    var = jnp.mean(jnp.square(xf - mu), -1, keepdims=True)
    return ((xf - mu) * lax.rsqrt(var + LN_EPS) * g + b).astype(x.dtype)


def rms_norm(x, g):
    xf = x.astype(jnp.float32)
    return (xf * lax.rsqrt(jnp.mean(xf * xf, -1, keepdims=True) + RMS_EPS) * g).astype(x.dtype)


def project_in(h, w):
    B, T, _ = h.shape
    z = jnp.einsum('btd,de->bte', h, w)
    u_a = z[..., :D_GMLP]
    v_a = z[..., D_GMLP:OFF_Q]
    q = z[..., OFF_Q:OFF_K].reshape(B, T, N_DIFF_HEADS, 2 * DIFF_QK_DIM)
    k = z[..., OFF_K:OFF_V].reshape(B, T, N_DIFF_HEADS, 2 * DIFF_QK_DIM)
    v_b = z[..., OFF_V:].reshape(B, T, N_DIFF_HEADS, DIFF_V_DIM)
    return u_a, v_a, q, k, v_b


def block_causal_mask(n, dtype):
    i = jnp.arange(n)
    return ((i[None, :] // CHUNK) <= (i[:, None] // CHUNK)).astype(dtype)


def sgu_prompt(u, v_n, w_s, b_s):
    B, S, _ = u.shape
    vb = v_n.reshape(B, S // GMLP_BLOCK, GMLP_BLOCK, N_GMLP_HEADS, GMLP_HEAD)
    w = w_s * block_causal_mask(GMLP_BLOCK, w_s.dtype)[None]
    s = jnp.einsum('gij,bnjgc->bnigc', w, vb) + b_s.T[:, :, None]
    return u * s.reshape(B, S, D_GMLP)


def sgu_sample(u, v_n, w_s, b_s):
    B, T, _ = u.shape
    vb = v_n.reshape(B, T, N_GMLP_HEADS, GMLP_HEAD)
    w = (w_s * block_causal_mask(GMLP_BLOCK, w_s.dtype)[None])[:, :T, :T]
    s = jnp.einsum('gij,bjgc->bigc', w, vb) + b_s.T[:T, :, None]
    return u * s.reshape(B, T, D_GMLP)


def chunk_alibi_bias(t_pos, s_pos):
    slopes = jnp.asarray(ALIBI_SLOPES, jnp.float32)
    dist = jnp.abs(t_pos[:, None] - s_pos[None, :]).astype(jnp.float32)
    vis = (s_pos[None, :] // CHUNK) <= (t_pos[:, None] // CHUNK)
    return jnp.where(vis[None], -slopes[:, None, None] * dist[None], -jnp.inf)


def diff_attend(q, k, v, bias, lam):
    scale = DIFF_QK_DIM ** -0.5
    q1, q2 = q[..., :DIFF_QK_DIM], q[..., DIFF_QK_DIM:]
    k1, k2 = k[..., :DIFF_QK_DIM], k[..., DIFF_QK_DIM:]
    s1 = jnp.einsum('bqhd,bkhd->bhqk', q1, k1, preferred_element_type=jnp.float32) * scale + bias
    s2 = jnp.einsum('bqhd,bkhd->bhqk', q2, k2, preferred_element_type=jnp.float32) * scale + bias
    a = jax.nn.softmax(s1, axis=-1) - lam * jax.nn.softmax(s2, axis=-1)
    return jnp.einsum('bhqk,bkhd->bqhd', a.astype(v.dtype), v)


def diff_attn_prompt(q, k, v, lam):
    B, S = q.shape[:2]
    nqb = S // Q_BLOCK
    qb = q.reshape(B, nqb, Q_BLOCK, N_DIFF_HEADS, 2 * DIFF_QK_DIM).transpose(1, 0, 2, 3, 4)
    s_pos = jnp.arange(S)

    def one_block(args):
        i, q_blk = args
        t_pos = i * Q_BLOCK + jnp.arange(Q_BLOCK)
        return diff_attend(q_blk, k, v, chunk_alibi_bias(t_pos, s_pos), lam)

    out = lax.map(one_block, (jnp.arange(nqb), qb))
    return out.transpose(1, 0, 2, 3, 4).reshape(B, S, N_DIFF_HEADS, DIFF_V_DIM)


def diff_attn_sample(q, k_all, v_all, lam, past):
    T = q.shape[1]
    t_pos = past + jnp.arange(T)
    s_pos = jnp.arange(past + T)
    return diff_attend(q, k_all, v_all, chunk_alibi_bias(t_pos, s_pos), lam)


def diff_post(o, g, lam_init):
    o = rms_norm(o, g) * (1.0 - lam_init)
    return o.reshape(o.shape[:-2] + (D_DIFF,))


def route(x, w_router, router_bias):
    T = x.shape[0]
    aff = jax.nn.sigmoid(jnp.einsum('td,de->te', x, w_router, preferred_element_type=jnp.float32))
    sel = aff + router_bias.astype(jnp.float32)
    sel_g = sel.reshape(T, N_GROUPS, EXPERTS_PER_GROUP)
    group_score = jnp.sum(lax.top_k(sel_g, TOP_K)[0], axis=-1)
    g_idx = jnp.argmax(group_score, axis=-1)
    gmask = g_idx[:, None] == jnp.arange(N_GROUPS)[None, :]
    masked = jnp.where(gmask[:, :, None], sel_g, -jnp.inf).reshape(T, N_EXPERTS)
    _, e_idx = lax.top_k(masked, TOP_K)
    w = jnp.take_along_axis(aff, e_idx, axis=-1)
    w = w / jnp.sum(w, axis=-1, keepdims=True)
    return jnp.sum(jax.nn.one_hot(e_idx, N_EXPERTS, dtype=jnp.float32) * w[..., None], axis=-2)


def moe_ffn(x, gates, w_up, b_up, w_down, b_down):
    g = gates.astype(x.dtype)
    h = jax.nn.gelu(jnp.einsum('td,edf->tef', x, w_up) + b_up) * g[:, :, None]
    return jnp.einsum('tef,efd->td', h, w_down) + jnp.einsum('te,ed->td', g, b_down)


def block_tail(h, o_a, o_b, w_out, b_out, ln_mix_g, ln_mix_b, w_router, router_bias,
               w_up, b_up, w_down, b_down, ln_ffn_g, ln_ffn_b):
    o = jnp.concatenate([o_a, o_b], axis=-1)
    mix = jnp.einsum('btc,cd->btd', o, w_out) + b_out
    h = layer_norm(ALPHA * h + mix, ln_mix_g, ln_mix_b)
    B, T, D = h.shape
    hf = h.reshape(B * T, D)
    gates = route(hf, w_router, router_bias)
    f = moe_ffn(hf, gates, w_up, b_up, w_down, b_down).reshape(B, T, D)
    return layer_norm(ALPHA * h + f, ln_ffn_g, ln_ffn_b)


def setup_inputs(seed: int = 0) -> dict:
    key = jax.random.key(seed)
    ks = jax.random.split(key, 32)
    f32 = jnp.float32
    nrm = lambda k, shape: jax.random.normal(k, shape, f32)
    col_scale = jnp.concatenate([jnp.ones((OFF_V,), f32), jnp.full((D_DIFF,), BETA, f32)])
    return {
        'x_prompt': nrm(ks[0], (BATCH, SEQ, D_MODEL)),
        'x_sample': nrm(ks[1], (DEC_BATCH, DEC_SEQ, D_MODEL)),
        'cache_k': nrm(ks[2], (DEPTH, DEC_BATCH, PAST_LEN, N_DIFF_HEADS, 2 * DIFF_QK_DIM)),
        'cache_v': BETA * nrm(ks[3], (DEPTH, DEC_BATCH, PAST_LEN, N_DIFF_HEADS, DIFF_V_DIM)),
        'ln_in_g': 1.0 + 0.02 * nrm(ks[4], (D_MODEL,)),
        'ln_in_b': 0.02 * nrm(ks[5], (D_MODEL,)),
        'w_in': nrm(ks[6], (DEPTH, D_MODEL, D_IN)) * D_MODEL ** -0.5 * col_scale,
        'w_out': BETA * D_MIX ** -0.5 * nrm(ks[7], (DEPTH, D_MIX, D_MODEL)),
        'b_out': 0.02 * nrm(ks[8], (DEPTH, D_MODEL)),
        'gmlp_ln_g': 1.0 + 0.02 * nrm(ks[9], (DEPTH, D_GMLP)),
        'gmlp_ln_b': 0.02 * nrm(ks[10], (DEPTH, D_GMLP)),
        'gmlp_w_s': GMLP_BLOCK ** -0.5 * nrm(ks[11], (DEPTH, N_GMLP_HEADS, GMLP_BLOCK, GMLP_BLOCK)),
        'gmlp_b_s': 1.0 + 0.02 * nrm(ks[12], (DEPTH, N_GMLP_HEADS, GMLP_BLOCK)),
        'lambda_q1': 0.1 * nrm(ks[13], (DEPTH, DIFF_QK_DIM)),
        'lambda_k1': 0.1 * nrm(ks[14], (DEPTH, DIFF_QK_DIM)),
        'lambda_q2': 0.1 * nrm(ks[15], (DEPTH, DIFF_QK_DIM)),
        'lambda_k2': 0.1 * nrm(ks[16], (DEPTH, DIFF_QK_DIM)),
        'subln_g': 1.0 + 0.02 * nrm(ks[17], (DEPTH, DIFF_V_DIM)),
        'ln_mix_g': 1.0 + 0.02 * nrm(ks[18], (DEPTH, D_MODEL)),
        'ln_mix_b': 0.02 * nrm(ks[19], (DEPTH, D_MODEL)),
        'w_router': D_MODEL ** -0.5 * nrm(ks[20], (D_MODEL, N_EXPERTS)),
        'router_bias': 0.01 * nrm(ks[21], (N_EXPERTS,)),
        'w_up': D_MODEL ** -0.5 * nrm(ks[22], (DEPTH, N_EXPERTS, D_MODEL, D_EXPERT)),
        'b_up': 0.02 * nrm(ks[23], (DEPTH, N_EXPERTS, D_EXPERT)),
        'w_down': BETA * D_EXPERT ** -0.5 * nrm(ks[24], (DEPTH, N_EXPERTS, D_EXPERT, D_MODEL)),
        'b_down': 0.02 * nrm(ks[25], (DEPTH, N_EXPERTS, D_MODEL)),
        'ln_ffn_g': 1.0 + 0.02 * nrm(ks[26], (DEPTH, D_MODEL)),
        'ln_ffn_b': 0.02 * nrm(ks[27], (DEPTH, D_MODEL)),
    }


def reference(x_prompt, x_sample, cache_k, cache_v, ln_in_g, ln_in_b, w_in, w_out, b_out,
              gmlp_ln_g, gmlp_ln_b, gmlp_w_s, gmlp_b_s, lambda_q1, lambda_k1, lambda_q2,
              lambda_k2, subln_g, ln_mix_g, ln_mix_b, w_router, router_bias, w_up, b_up,
              w_down, b_down, ln_ffn_g, ln_ffn_b):
    past = cache_k.shape[2]
    hp = layer_norm(x_prompt, ln_in_g, ln_in_b)
    hs = layer_norm(x_sample, ln_in_g, ln_in_b)
    kp, vp, ksm, vsm, gvs = [], [], [], [], []
    for l in range(DEPTH):
        lam_init = 0.8 - 0.6 * math.exp(-0.3 * l)
        lam = (jnp.exp(jnp.sum(lambda_q1[l].astype(jnp.float32) * lambda_k1[l].astype(jnp.float32)))
               - jnp.exp(jnp.sum(lambda_q2[l].astype(jnp.float32) * lambda_k2[l].astype(jnp.float32)))
               + lam_init)
        tail = (w_out[l], b_out[l], ln_mix_g[l], ln_mix_b[l], w_router, router_bias,
                w_up[l], b_up[l], w_down[l], b_down[l], ln_ffn_g[l], ln_ffn_b[l])

        u_a, v_a, q, k, v = project_in(hp, w_in[l])
        v_n = layer_norm(v_a, gmlp_ln_g[l], gmlp_ln_b[l])
        o_a = sgu_prompt(u_a, v_n, gmlp_w_s[l], gmlp_b_s[l])
        o_b = diff_post(diff_attn_prompt(q, k, v, lam), subln_g[l], lam_init)
        hp = block_tail(hp, o_a, o_b, *tail)
        kp.append(k)
        vp.append(v)

        u_a, v_a, q, k, v = project_in(hs, w_in[l])
        v_n = layer_norm(v_a, gmlp_ln_g[l], gmlp_ln_b[l])
        o_a = sgu_sample(u_a, v_n, gmlp_w_s[l], gmlp_b_s[l])
        k_all = jnp.concatenate([cache_k[l], k], axis=1)
        v_all = jnp.concatenate([cache_v[l], v], axis=1)
        o_b = diff_post(diff_attn_sample(q, k_all, v_all, lam, past), subln_g[l], lam_init)
        hs = block_tail(hs, o_a, o_b, *tail)
        ksm.append(k)
        vsm.append(v)
        gvs.append(v_n)

    k_prompt = jnp.stack(kp)
    v_prompt = jnp.stack(vp)
    k_sample = jnp.stack(ksm)
    v_sample = jnp.stack(vsm)
    gmlp_v_sample = jnp.stack(gvs)
    return (hp, hs, k_prompt, v_prompt, k_sample, v_sample, gmlp_v_sample)
```

```python
import functools
import math

import jax
import jax.numpy as jnp
from jax import lax
from jax.experimental import pallas as pl
from jax.experimental.pallas import tpu as pltpu

F32 = jnp.float32
BF16 = jnp.bfloat16

D_MODEL = 1024
DEPTH = 4
CHUNK = 64
D_GMLP = 512
GMLP_BLOCK = 128
GMLP_HEAD = 128
N_GMLP_HEADS = 4
D_DIFF = 512
N_DIFF_HEADS = 4
DIFF_V_DIM = 128
DIFF_QK_DIM = 64
OFF_Q = 2 * D_GMLP
OFF_K = OFF_Q + 512
OFF_V = OFF_K + 512
D_IN = OFF_V + D_DIFF
N_EXPERTS = 16
N_GROUPS = 4
EXPERTS_PER_GROUP = 4
D_EXPERT = 512
ALPHA = (2.0 * DEPTH) ** 0.25
LN_EPS = 1e-5
RMS_EPS = 1e-5
QK_SCALE = DIFF_QK_DIM ** -0.5
LANES = 128
VMEM_LIMIT = 48 * 1024 * 1024


def _ln(x, g, b):
    mu = jnp.mean(x, -1, keepdims=True)
    xc = x - mu
    var = jnp.mean(xc * xc, -1, keepdims=True)
    return xc * lax.rsqrt(var + LN_EPS) * g + b


def _params(sem):
    return pltpu.CompilerParams(dimension_semantics=sem, vmem_limit_bytes=VMEM_LIMIT)


def _ln_kernel(x_ref, g_ref, b_ref, o_ref):
    o_ref[...] = _ln(x_ref[...], g_ref[...], b_ref[...])


def _input_ln(x, g, b, tm):
    t, d = x.shape
    return pl.pallas_call(
        _ln_kernel,
        grid=(t // tm,),
        in_specs=[pl.BlockSpec((tm, d), lambda i: (i, 0)),
                  pl.BlockSpec((1, d), lambda i: (0, 0)),
                  pl.BlockSpec((1, d), lambda i: (0, 0))],
        out_specs=pl.BlockSpec((tm, d), lambda i: (i, 0)),
        out_shape=jax.ShapeDtypeStruct((t, d), F32),
        compiler_params=_params(("parallel",)),
        name="input_ln",
    )(x, g.reshape(1, d), b.reshape(1, d))


def _proj_kernel(h_ref, w_ref, lng_ref, lnb_ref, ws_ref, bs_ref,
                 oa_ref, q_ref, kf_ref, vf_ref, kb_ref, vb_ref, vn_ref, *, tm, blk):
    hb = h_ref[...].astype(BF16)

    def proj(off, width):
        return jnp.dot(hb, w_ref[:, off:off + width], preferred_element_type=F32)

    u = proj(0, D_GMLP)
    v_a = proj(D_GMLP, D_GMLP)
    q = proj(OFF_Q, 512)
    k = proj(OFF_K, 512)
    v = proj(OFF_V, D_DIFF)

    q_ref[...] = (q * QK_SCALE).astype(BF16)
    kf_ref[...] = k
    vf_ref[...] = v
    kb_ref[...] = k.astype(BF16)
    vb_ref[...] = v.astype(BF16)

    v_n = _ln(v_a, lng_ref[...], lnb_ref[...])
    vn_ref[...] = v_n
    v_nb = v_n.astype(BF16)

    ri = lax.broadcasted_iota(jnp.int32, (blk, blk), 0) // CHUNK
    ci = lax.broadcasted_iota(jnp.int32, (blk, blk), 1) // CHUNK
    vis = ci <= ri
    for g in range(N_GMLP_HEADS):
        w_g = jnp.where(vis, ws_ref[g, :blk, :blk], 0.0).astype(BF16)
        b_g = bs_ref[:blk, g:g + 1]
        c0 = g * GMLP_HEAD
        for r in range(tm // blk):
            r0 = r * blk
            s = jnp.dot(w_g, v_nb[r0:r0 + blk, c0:c0 + GMLP_HEAD],
                        preferred_element_type=F32) + b_g
            oa_ref[r0:r0 + blk, c0:c0 + GMLP_HEAD] = (
                u[r0:r0 + blk, c0:c0 + GMLP_HEAD] * s).astype(BF16)


def _proj(h, w_in_b, ln_g, ln_b, w_s, b_s_t, tm, blk):
    t = h.shape[0]
    row = lambda i: (i, 0)
    const2 = lambda i: (0, 0)
    outs = pl.pallas_call(
        functools.partial(_proj_kernel, tm=tm, blk=blk),
        grid=(t // tm,),
        in_specs=[pl.BlockSpec((tm, D_MODEL), row),
                  pl.BlockSpec((D_MODEL, D_IN), const2),
                  pl.BlockSpec((1, D_GMLP), const2),
                  pl.BlockSpec((1, D_GMLP), const2),
                  pl.BlockSpec((N_GMLP_HEADS, GMLP_BLOCK, GMLP_BLOCK), lambda i: (0, 0, 0)),
                  pl.BlockSpec((GMLP_BLOCK, N_GMLP_HEADS), const2)],
        out_specs=[pl.BlockSpec((tm, 512), row)] * 7,
        out_shape=[jax.ShapeDtypeStruct((t, 512), BF16),
                   jax.ShapeDtypeStruct((t, 512), BF16),
                   jax.ShapeDtypeStruct((t, 512), F32),
                   jax.ShapeDtypeStruct((t, 512), F32),
                   jax.ShapeDtypeStruct((t, 512), BF16),
                   jax.ShapeDtypeStruct((t, 512), BF16),
                   jax.ShapeDtypeStruct((t, 512), F32)],
        compiler_params=_params(("parallel",)),
        name="proj_gmlp",
    )(h, w_in_b, ln_g.reshape(1, -1), ln_b.reshape(1, -1), w_s, b_s_t)
    return outs


def _lambda(lamp_ref, lam_init):
    lp = lamp_ref[...]
    d1 = jnp.sum(lp[0:1] * lp[1:2], axis=-1, keepdims=True)
    d2 = jnp.sum(lp[2:3] * lp[3:4], axis=-1, keepdims=True)
    return jnp.exp(d1) - jnp.exp(d2) + lam_init


def _head_slope(h):
    return jnp.exp2(-(8.0 / N_DIFF_HEADS) * (h + 1).astype(F32))


def _split_q(q):
    lane = lax.broadcasted_iota(jnp.int32, q.shape, 1)
    zero = jnp.zeros_like(q)
    return jnp.where(lane < DIFF_QK_DIM, q, zero), jnp.where(lane >= DIFF_QK_DIM, q, zero)


def _qk(qz, kc):
    return lax.dot_general(qz, kc, (((1,), (1,)), ((), ())), preferred_element_type=F32)


def _diff_post(o1, o2, lam, g, lam_init):
    o = o1 - lam * o2
    o = o * lax.rsqrt(jnp.mean(o * o, -1, keepdims=True) + RMS_EPS) * g
    return o * (1.0 - lam_init)


def _attn_prompt_kernel(q_ref, k_ref, v_ref, lamp_ref, g_ref, o_ref,
                        m1, l1, a1, m2, l2, a2, *, tq, lam_init):
    h = pl.program_id(0)
    i = pl.program_id(1)
    slope = _head_slope(h)
    q1, q2 = _split_q(q_ref[...])
    for m, l, a in ((m1, l1, a1), (m2, l2, a2)):
        m[...] = jnp.full(m.shape, -jnp.inf, F32)
        l[...] = jnp.zeros(l.shape, F32)
        a[...] = jnp.zeros(a.shape, F32)
    t_pos = i * tq + lax.broadcasted_iota(jnp.int32, (tq, 1), 0)

    def body(j, carry):
        start = pl.multiple_of(j * tq, tq)
        kc = k_ref[pl.ds(start, tq), :]
        vc = v_ref[pl.ds(start, tq), :]
        s_pos = j * tq + lax.broadcasted_iota(jnp.int32, (1, tq), 1)
        dist = jnp.abs(t_pos - s_pos).astype(F32)
        vis = (s_pos // CHUNK) <= (t_pos // CHUNK)
        bias = jnp.where(vis, -slope * dist, -jnp.inf)
        for qz, m, l, a in ((q1, m1, l1, a1), (q2, m2, l2, a2)):
            s = _qk(qz, kc) + bias
            m_prev = m[...]
            m_new = jnp.maximum(m_prev, jnp.max(s, axis=-1, keepdims=True))
            alpha = jnp.exp(m_prev - m_new)
            p = jnp.exp(s - m_new)
            l[...] = alpha * l[...] + jnp.sum(p, axis=-1, keepdims=True)
            a[...] = alpha * a[...] + jnp.dot(p.astype(BF16), vc, preferred_element_type=F32)
            m[...] = m_new
        return carry

    lax.fori_loop(0, i + 1, body, 0)
    lam = _lambda(lamp_ref, lam_init)
    o_ref[...] = _diff_post(a1[...] / l1[...], a2[...] / l2[...], lam, g_ref[...],
                            lam_init).astype(o_ref.dtype)


def _attn_prompt(q, kb, vb, lamp, g, lam_init, tq):
    s = q.shape[0]
    return pl.pallas_call(
        functools.partial(_attn_prompt_kernel, tq=tq, lam_init=lam_init),
        grid=(N_DIFF_HEADS, s // tq),
        in_specs=[pl.BlockSpec((tq, DIFF_V_DIM), lambda h, i: (i, h)),
                  pl.BlockSpec((s, DIFF_V_DIM), lambda h, i: (0, h)),
                  pl.BlockSpec((s, DIFF_V_DIM), lambda h, i: (0, h)),
                  pl.BlockSpec((4, DIFF_QK_DIM), lambda h, i: (0, 0)),
                  pl.BlockSpec((1, DIFF_V_DIM), lambda h, i: (0, 0))],
        out_specs=pl.BlockSpec((tq, DIFF_V_DIM), lambda h, i: (i, h)),
        out_shape=jax.ShapeDtypeStruct((s, D_DIFF), BF16),
        scratch_shapes=[pltpu.VMEM((tq, 1), F32), pltpu.VMEM((tq, 1), F32),
                        pltpu.VMEM((tq, DIFF_V_DIM), F32)] * 2,
        compiler_params=_params(("parallel", "arbitrary")),
        name="attn_prompt",
    )(q, kb, vb, lamp, g)


def _attn_sample_kernel(q_ref, kn_ref, vn_ref, ck_ref, cv_ref, lamp_ref, g_ref, o_ref,
                        *, t, past, lam_init):
    h = pl.program_id(1)
    slope = _head_slope(h)
    q1, q2 = _split_q(q_ref[...])
    kc = ck_ref[...].astype(BF16)
    vc = cv_ref[...].astype(BF16)
    kn = kn_ref[...]
    vn = vn_ref[...]
    t_pos = past + lax.broadcasted_iota(jnp.int32, (t, 1), 0)

    def bias(s_pos):
        dist = jnp.abs(t_pos - s_pos).astype(F32)
        vis = (s_pos // CHUNK) <= (t_pos // CHUNK)
        return jnp.where(vis, -slope * dist, -jnp.inf)

    bias_c = bias(lax.broadcasted_iota(jnp.int32, (1, past), 1))
    bias_n = bias(past + lax.broadcasted_iota(jnp.int32, (1, t), 1))

    def one_map(qz):
        sc = _qk(qz, kc) + bias_c
        sn = _qk(qz, kn) + bias_n
        m = jnp.maximum(jnp.max(sc, axis=-1, keepdims=True), jnp.max(sn, axis=-1, keepdims=True))
        pc = jnp.exp(sc - m)
        pn = jnp.exp(sn - m)
        l = jnp.sum(pc, axis=-1, keepdims=True) + jnp.sum(pn, axis=-1, keepdims=True)
        o = (jnp.dot(pc.astype(BF16), vc, preferred_element_type=F32)
             + jnp.dot(pn.astype(BF16), vn, preferred_element_type=F32))
        return o / l

    lam = _lambda(lamp_ref, lam_init)
    o_ref[...] = _diff_post(one_map(q1), one_map(q2), lam, g_ref[...],
                            lam_init).astype(o_ref.dtype)


def _attn_sample(q, kb, vb, cache_k, cache_v, layer, lamp, g, lam_init, nb, t):
    past = cache_k.shape[2]
    return pl.pallas_call(
        functools.partial(_attn_sample_kernel, t=t, past=past, lam_init=lam_init),
        grid=(nb, N_DIFF_HEADS),
        in_specs=[pl.BlockSpec((t, DIFF_V_DIM), lambda b, h: (b, h)),
                  pl.BlockSpec((t, DIFF_V_DIM), lambda b, h: (b, h)),
                  pl.BlockSpec((t, DIFF_V_DIM), lambda b, h: (b, h)),
                  pl.BlockSpec((None, None, past, DIFF_V_DIM), lambda b, h: (layer, b, 0, h)),
                  pl.BlockSpec((None, None, past, DIFF_V_DIM), lambda b, h: (layer, b, 0, h)),
                  pl.BlockSpec((4, DIFF_QK_DIM), lambda b, h: (0, 0)),
                  pl.BlockSpec((1, DIFF_V_DIM), lambda b, h: (0, 0))],
        out_specs=pl.BlockSpec((t, DIFF_V_DIM), lambda b, h: (b, h)),
        out_shape=jax.ShapeDtypeStruct((nb * t, D_DIFF), BF16),
        compiler_params=_params(("parallel", "parallel")),
        name="attn_sample",
    )(q, kb, vb, cache_k, cache_v, lamp, g)


def _split_bf16(x):
    hi = x.astype(BF16)
    lo = (x - hi.astype(F32)).astype(BF16)
    return hi, lo


def _top2_sum(a, b, c, d):
    return jnp.maximum(jnp.maximum(jnp.maximum(a + b, a + c), jnp.maximum(a + d, b + c)),
                       jnp.maximum(b + d, c + d))


def _gates_t(aff, sel):
    rows = [sel[e:e + 1, :] for e in range(N_EXPERTS)]
    arow = [aff[e:e + 1, :] for e in range(N_EXPERTS)]
    gs = [_top2_sum(*rows[4 * g:4 * g + 4]) for g in range(N_GROUPS)]
    best = jnp.maximum(jnp.maximum(gs[0], gs[1]), jnp.maximum(gs[2], gs[3]))
    is_g = []
    taken = None
    for g in range(N_GROUPS):
        hit = gs[g] == best
        if taken is None:
            is_g.append(hit)
            taken = hit
        else:
            is_g.append(hit & ~taken)
            taken = taken | hit
    vals, affs = [], []
    for i in range(EXPERTS_PER_GROUP):
        v = rows[i]
        a = arow[i]
        for g in range(1, N_GROUPS):
            v = jnp.where(is_g[g], rows[4 * g + i], v)
            a = jnp.where(is_g[g], arow[4 * g + i], a)
        vals.append(v)
        affs.append(a)
    picked = []
    for i in range(EXPERTS_PER_GROUP):
        rank = jnp.zeros_like(vals[i])
        for j in range(EXPERTS_PER_GROUP):
            if j == i:
                continue
            ahead = (vals[j] >= vals[i]) if j < i else (vals[j] > vals[i])
            rank = rank + jnp.where(ahead, 1.0, 0.0)
        picked.append(rank < 2.0)
    wsum = jnp.zeros_like(affs[0])
    for i in range(EXPERTS_PER_GROUP):
        wsum = wsum + jnp.where(picked[i], affs[i], 0.0)
    row_id = lax.broadcasted_iota(jnp.int32, aff.shape, 0)
    gates = jnp.zeros(aff.shape, F32)
    for g in range(N_GROUPS):
        for i in range(EXPERTS_PER_GROUP):
            w = jnp.where(is_g[g] & picked[i], affs[i] / wsum, 0.0)
            gates = jnp.where(row_id == 4 * g + i, w, gates)
    return gates


def _mix_kernel(h_ref, oa_ref, ob_ref, wo_ref, bo_ref, g_ref, b_ref, wr_hi_ref, wr_lo_ref,
                rb_ref, h1_ref, h1b_ref, gates_ref, *, tm):
    mix = (jnp.dot(oa_ref[...], wo_ref[:D_GMLP, :], preferred_element_type=F32)
           + jnp.dot(ob_ref[...], wo_ref[D_GMLP:, :], preferred_element_type=F32)
           + bo_ref[...])
    h1 = _ln(ALPHA * h_ref[...] + mix, g_ref[...], b_ref[...])
    h1_ref[...] = h1
    h1b_ref[...] = h1.astype(BF16)
    hi, lo = _split_bf16(h1)
    logits = (jnp.dot(hi, wr_hi_ref[...], preferred_element_type=F32)
              + jnp.dot(lo, wr_hi_ref[...], preferred_element_type=F32)
              + jnp.dot(hi, wr_lo_ref[...], preferred_element_type=F32))
    lt = logits.T[:N_EXPERTS, :]
    aff = 1.0 / (1.0 + jnp.exp(-lt))
    sel = aff + rb_ref[...]
    gates = _gates_t(aff, sel)
    gpad = jnp.concatenate([gates, jnp.zeros((LANES - N_EXPERTS, tm), F32)], axis=0)
    gates_ref[...] = gpad.T


def _mix(h, o_a, o_b, w_out_b, b_out, ln_g, ln_b, wr_hi, wr_lo, rb, tm):
    t = h.shape[0]
    row = lambda i: (i, 0)
    const2 = lambda i: (0, 0)
    return pl.pallas_call(
        functools.partial(_mix_kernel, tm=tm),
        grid=(t // tm,),
        in_specs=[pl.BlockSpec((tm, D_MODEL), row),
                  pl.BlockSpec((tm, D_GMLP), row),
                  pl.BlockSpec((tm, D_DIFF), row),
                  pl.BlockSpec((D_MODEL, D_MODEL), const2),
                  pl.BlockSpec((1, D_MODEL), const2),
                  pl.BlockSpec((1, D_MODEL), const2),
                  pl.BlockSpec((1, D_MODEL), const2),
                  pl.BlockSpec((D_MODEL, LANES), const2),
                  pl.BlockSpec((D_MODEL, LANES), const2),
                  pl.BlockSpec((N_EXPERTS, 1), const2)],
        out_specs=[pl.BlockSpec((tm, D_MODEL), row),
                   pl.BlockSpec((tm, D_MODEL), row),
                   pl.BlockSpec((tm, LANES), row)],
        out_shape=[jax.ShapeDtypeStruct((t, D_MODEL), F32),
                   jax.ShapeDtypeStruct((t, D_MODEL), BF16),
                   jax.ShapeDtypeStruct((t, LANES), F32)],
        compiler_params=_params(("parallel",)),
        name="mix_router",
    )(h, o_a, o_b, w_out_b, b_out.reshape(1, -1), ln_g.reshape(1, -1), ln_b.reshape(1, -1),
      wr_hi, wr_lo, rb)


def _gelu_tanh(x):
    return 0.5 * x * (1.0 + jnp.tanh(math.sqrt(2.0 / math.pi) * (x + 0.044715 * (x * x * x))))


def _moe_kernel(x_ref, gates_ref, h1_ref, wu_ref, bu_ref, wd_ref, bd_ref, g_ref, b_ref,
                o_ref, acc_ref):
    e = pl.program_id(1)

    @pl.when(e == 0)
    def _():
        acc_ref[...] = jnp.zeros(acc_ref.shape, F32)

    gates = gates_ref[...]
    lane = lax.broadcasted_iota(jnp.int32, gates.shape, 1)
    g_e = jnp.sum(jnp.where(lane == e, gates, 0.0), axis=-1, keepdims=True)
    up = jnp.dot(x_ref[...], wu_ref[...], preferred_element_type=F32) + bu_ref[...]
    hid = (_gelu_tanh(up) * g_e).astype(BF16)
    acc_ref[...] += jnp.dot(hid, wd_ref[...], preferred_element_type=F32)

    @pl.when(e == N_EXPERTS - 1)
    def _():
        f = acc_ref[...] + jnp.dot(gates.astype(BF16), bd_ref[...], preferred_element_type=F32)
        o_ref[...] = _ln(ALPHA * h1_ref[...] + f, g_ref[...], b_ref[...])


def _moe(h1, h1b, gates, w_up_b, b_up, w_down_b, b_down_pad, ln_g, ln_b, tm):
    t = h1.shape[0]
    row = lambda i, e: (i, 0)
    const2 = lambda i, e: (0, 0)
    return pl.pallas_call(
        _moe_kernel,
        grid=(t // tm, N_EXPERTS),
        in_specs=[pl.BlockSpec((tm, D_MODEL), row),
                  pl.BlockSpec((tm, LANES), row),
                  pl.BlockSpec((tm, D_MODEL), row),
                  pl.BlockSpec((None, D_MODEL, D_EXPERT), lambda i, e: (e, 0, 0)),
                  pl.BlockSpec((None, 1, D_EXPERT), lambda i, e: (e, 0, 0)),
                  pl.BlockSpec((None, D_EXPERT, D_MODEL), lambda i, e: (e, 0, 0)),
                  pl.BlockSpec((LANES, D_MODEL), const2),
                  pl.BlockSpec((1, D_MODEL), const2),
                  pl.BlockSpec((1, D_MODEL), const2)],
        out_specs=pl.BlockSpec((tm, D_MODEL), row),
        out_shape=jax.ShapeDtypeStruct((t, D_MODEL), F32),
        scratch_shapes=[pltpu.VMEM((tm, D_MODEL), F32)],
        compiler_params=_params(("parallel", "arbitrary")),
        name="moe_ln",
    )(h1b, gates, h1, w_up_b, b_up.reshape(N_EXPERTS, 1, D_EXPERT), w_down_b, b_down_pad,
      ln_g.reshape(1, -1), ln_b.reshape(1, -1))


def kernel(x_prompt, x_sample, cache_k, cache_v, ln_in_g, ln_in_b, w_in, w_out, b_out,
           gmlp_ln_g, gmlp_ln_b, gmlp_w_s, gmlp_b_s, lambda_q1, lambda_k1, lambda_q2,
           lambda_k2, subln_g, ln_mix_g, ln_mix_b, w_router, router_bias, w_up, b_up,
           w_down, b_down, ln_ffn_g, ln_ffn_b):
    nbp, seq, d = x_prompt.shape
    nbs, tdec, _ = x_sample.shape
    assert nbp == 1 and d == D_MODEL
    past = cache_k.shape[2]
    ts = nbs * tdec
    ck = cache_k.reshape(DEPTH, nbs, past, N_DIFF_HEADS * DIFF_V_DIM)
    cv = cache_v.reshape(DEPTH, nbs, past, N_DIFF_HEADS * DIFF_V_DIM)

    hp = _input_ln(x_prompt.reshape(seq, d), ln_in_g, ln_in_b, 1024)
    hs = _input_ln(x_sample.reshape(ts, d), ln_in_g, ln_in_b, ts)

    wr_pad = jnp.pad(w_router, ((0, 0), (0, LANES - N_EXPERTS)))
    wr_hi = wr_pad.astype(BF16)
    wr_lo = (wr_pad - wr_hi.astype(F32)).astype(BF16)
    rb = router_bias.astype(F32).reshape(N_EXPERTS, 1)

    kp, vp, ksm, vsm, gvs = [], [], [], [], []
    for l in range(DEPTH):
        lam_init = 0.8 - 0.6 * math.exp(-0.3 * l)
        w_in_b = w_in[l].astype(BF16)
        w_out_b = w_out[l].astype(BF16)
        w_up_b = w_up[l].astype(BF16)
        w_down_b = w_down[l].astype(BF16)
        b_down_pad = jnp.pad(b_down[l], ((0, LANES - N_EXPERTS), (0, 0))).astype(BF16)
        b_s_t = gmlp_b_s[l].T
        lamp = jnp.stack([lambda_q1[l], lambda_k1[l], lambda_q2[l], lambda_k2[l]]).astype(F32)
        g_sub = subln_g[l].reshape(1, DIFF_V_DIM)

        def tail(h, o_a, o_b, tm_mix, tm_moe):
            h1, h1b, gates = _mix(h, o_a, o_b, w_out_b, b_out[l], ln_mix_g[l], ln_mix_b[l],
                                  wr_hi, wr_lo, rb, tm_mix)
            return _moe(h1, h1b, gates, w_up_b, b_up[l], w_down_b, b_down_pad,
                        ln_ffn_g[l], ln_ffn_b[l], tm_moe)

        o_a, q, kf, vf, kb, vb, _ = _proj(hp, w_in_b, gmlp_ln_g[l], gmlp_ln_b[l],
                                          gmlp_w_s[l], b_s_t, 512, GMLP_BLOCK)
        o_b = _attn_prompt(q, kb, vb, lamp, g_sub, lam_init, 512)
        hp = tail(hp, o_a, o_b, 512, 1024)
        kp.append(kf)
        vp.append(vf)

        o_a, q, kf, vf, kb, vb, vn = _proj(hs, w_in_b, gmlp_ln_g[l], gmlp_ln_b[l],
                                           gmlp_w_s[l], b_s_t, ts, tdec)
        o_b = _attn_sample(q, kb, vb, ck, cv, l, lamp, g_sub, lam_init, nbs, tdec)
        hs = tail(hs, o_a, o_b, ts, ts)
        ksm.append(kf)
        vsm.append(vf)
        gvs.append(vn)

    hd = (N_DIFF_HEADS, DIFF_V_DIM)
    return (hp.reshape(1, seq, d),
            hs.reshape(nbs, tdec, d),
            jnp.stack(kp).reshape(DEPTH, 1, seq, *hd),
            jnp.stack(vp).reshape(DEPTH, 1, seq, *hd),
            jnp.stack(ksm).reshape(DEPTH, nbs, tdec, *hd),
            jnp.stack(vsm).reshape(DEPTH, nbs, tdec, *hd),
            jnp.stack(gvs).reshape(DEPTH, nbs, tdec, D_GMLP))
```

```python
import functools
import math

import jax
import jax.numpy as jnp
from jax import lax
from jax.experimental import pallas as pl
from jax.experimental.pallas import tpu as pltpu

F32 = jnp.float32
BF16 = jnp.bfloat16

D_MODEL = 1024
DEPTH = 4
CHUNK = 64
D_GMLP = 512
GMLP_BLOCK = 128
GMLP_HEAD = 128
N_GMLP_HEADS = 4
D_DIFF = 512
N_DIFF_HEADS = 4
DIFF_V_DIM = 128
DIFF_QK_DIM = 64
OFF_Q = 2 * D_GMLP
OFF_K = OFF_Q + 512
OFF_V = OFF_K + 512
D_IN = OFF_V + D_DIFF
N_EXPERTS = 16
N_GROUPS = 4
EXPERTS_PER_GROUP = 4
D_EXPERT = 512
ALPHA = (2.0 * DEPTH) ** 0.25
LN_EPS = 1e-5
RMS_EPS = 1e-5
QK_SCALE = DIFF_QK_DIM ** -0.5
LOG2E = math.log2(math.e)
ATTN_BLOCK = 512
N_BIAS = 3
LANES = 128
VMEM_LIMIT = 48 * 1024 * 1024


def _ln(x, g, b):
    mu = jnp.mean(x, -1, keepdims=True)
    xc = x - mu
    var = jnp.mean(xc * xc, -1, keepdims=True)
    return xc * lax.rsqrt(var + LN_EPS) * g + b


def _params(sem):
    return pltpu.CompilerParams(dimension_semantics=sem, vmem_limit_bytes=VMEM_LIMIT)


def _ln_kernel(x_ref, g_ref, b_ref, o_ref):
    o_ref[...] = _ln(x_ref[...], g_ref[...], b_ref[...])


def _input_ln(x, g, b, tm):
    t, d = x.shape
    return pl.pallas_call(
        _ln_kernel,
        grid=(t // tm,),
        in_specs=[pl.BlockSpec((tm, d), lambda i: (i, 0)),
                  pl.BlockSpec((1, d), lambda i: (0, 0)),
                  pl.BlockSpec((1, d), lambda i: (0, 0))],
        out_specs=pl.BlockSpec((tm, d), lambda i: (i, 0)),
        out_shape=jax.ShapeDtypeStruct((t, d), F32),
        compiler_params=_params(("parallel",)),
        name="input_ln",
    )(x, g.reshape(1, d), b.reshape(1, d))


def _sgu(u, v_nb, ws_ref, bs_ref, oa_ref, tm, blk):
    ri = lax.broadcasted_iota(jnp.int32, (blk, blk), 0) // CHUNK
    ci = lax.broadcasted_iota(jnp.int32, (blk, blk), 1) // CHUNK
    vis = ci <= ri
    for g in range(N_GMLP_HEADS):
        w_g = jnp.where(vis, ws_ref[g, :blk, :blk], 0.0).astype(BF16)
        b_g = bs_ref[:blk, g:g + 1]
        c0 = g * GMLP_HEAD
        for r in range(tm // blk):
            r0 = r * blk
            s = jnp.dot(w_g, v_nb[r0:r0 + blk, c0:c0 + GMLP_HEAD],
                        preferred_element_type=F32) + b_g
            oa_ref[r0:r0 + blk, c0:c0 + GMLP_HEAD] = (
                u[r0:r0 + blk, c0:c0 + GMLP_HEAD] * s).astype(BF16)


def _proj_sample_kernel(h_ref, w_ref, lng_ref, lnb_ref, ws_ref, bs_ref,
                        oa_ref, q_ref, kf_ref, vf_ref, kb_ref, vb_ref, vn_ref, *, tm, blk):
    hb = h_ref[...].astype(BF16)

    def proj(off, width):
        return jnp.dot(hb, w_ref[:, off:off + width], preferred_element_type=F32)

    u = proj(0, D_GMLP)
    v_a = proj(D_GMLP, D_GMLP)
    q = proj(OFF_Q, 512)
    k = proj(OFF_K, 512)
    v = proj(OFF_V, D_DIFF)
    q_ref[...] = (q * QK_SCALE).astype(BF16)
    kf_ref[...] = k
    vf_ref[...] = v
    kb_ref[...] = k.astype(BF16)
    vb_ref[...] = v.astype(BF16)
    v_n = _ln(v_a, lng_ref[...], lnb_ref[...])
    vn_ref[...] = v_n
    _sgu(u, v_n.astype(BF16), ws_ref, bs_ref, oa_ref, tm, blk)


def _proj_prompt_kernel(h_ref, w_ref, lng_ref, lnb_ref, ws_ref, bs_ref, kbias_ref,
                        oa_ref, qt1_ref, qt2_ref, ka1_ref, ka2_ref, vt_ref, kf_ref, vf_ref,
                        *, tm, blk):
    hb = h_ref[...].astype(BF16)

    def proj(off, width):
        return jnp.dot(hb, w_ref[:, off:off + width], preferred_element_type=F32)

    u = proj(0, D_GMLP)
    v_a = proj(D_GMLP, D_GMLP)
    q = proj(OFF_Q, 512)
    k = proj(OFF_K, 512)
    v = proj(OFF_V, D_DIFF)
    kf_ref[...] = k
    vf_ref[...] = v
    vt_ref[...] = v.T.astype(BF16)

    lane = lax.broadcasted_iota(jnp.int32, k.shape, 1) % DIFF_V_DIM
    kbias = kbias_ref[...]
    ka1_ref[...] = jnp.where(lane < DIFF_QK_DIM, k, kbias).astype(BF16)
    ka2_ref[...] = jnp.where(lane >= DIFF_QK_DIM, k, kbias).astype(BF16)

    qt = (q * (QK_SCALE * LOG2E)).T
    row = lax.broadcasted_iota(jnp.int32, qt.shape, 0) % DIFF_V_DIM
    ones1 = jnp.where((row >= DIFF_QK_DIM) & (row < DIFF_QK_DIM + N_BIAS), 1.0, 0.0)
    ones2 = jnp.where(row < N_BIAS, 1.0, 0.0)
    qt1_ref[...] = jnp.where(row < DIFF_QK_DIM, qt, ones1).astype(BF16)
    qt2_ref[...] = jnp.where(row >= DIFF_QK_DIM, qt, ones2).astype(BF16)

    v_n = _ln(v_a, lng_ref[...], lnb_ref[...])
    _sgu(u, v_n.astype(BF16), ws_ref, bs_ref, oa_ref, tm, blk)


def _proj_in_specs(tm):
    row = lambda i: (i, 0)
    const2 = lambda i: (0, 0)
    return [pl.BlockSpec((tm, D_MODEL), row),
            pl.BlockSpec((D_MODEL, D_IN), const2),
            pl.BlockSpec((1, D_GMLP), const2),
            pl.BlockSpec((1, D_GMLP), const2),
            pl.BlockSpec((N_GMLP_HEADS, GMLP_BLOCK, GMLP_BLOCK), lambda i: (0, 0, 0)),
            pl.BlockSpec((GMLP_BLOCK, N_GMLP_HEADS), const2)]


def _proj_sample(h, w_in_b, ln_g, ln_b, w_s, b_s_t, tm, blk):
    t = h.shape[0]
    row = lambda i: (i, 0)
    return pl.pallas_call(
        functools.partial(_proj_sample_kernel, tm=tm, blk=blk),
        grid=(t // tm,),
        in_specs=_proj_in_specs(tm),
        out_specs=[pl.BlockSpec((tm, 512), row)] * 7,
        out_shape=[jax.ShapeDtypeStruct((t, 512), BF16),
                   jax.ShapeDtypeStruct((t, 512), BF16),
                   jax.ShapeDtypeStruct((t, 512), F32),
                   jax.ShapeDtypeStruct((t, 512), F32),
                   jax.ShapeDtypeStruct((t, 512), BF16),
                   jax.ShapeDtypeStruct((t, 512), BF16),
                   jax.ShapeDtypeStruct((t, 512), F32)],
        compiler_params=_params(("parallel",)),
        name="proj_gmlp_sample",
    )(h, w_in_b, ln_g.reshape(1, -1), ln_b.reshape(1, -1), w_s, b_s_t)


def _proj_prompt(h, w_in_b, ln_g, ln_b, w_s, b_s_t, kbias, tm):
    t = h.shape[0]
    row = lambda i: (i, 0)
    col = lambda i: (0, i)
    rows_bf = jax.ShapeDtypeStruct((t, 512), BF16)
    cols_bf = jax.ShapeDtypeStruct((512, t), BF16)
    rows_f = jax.ShapeDtypeStruct((t, 512), F32)
    return pl.pallas_call(
        functools.partial(_proj_prompt_kernel, tm=tm, blk=GMLP_BLOCK),
        grid=(t // tm,),
        in_specs=_proj_in_specs(tm) + [pl.BlockSpec((tm, 512), lambda i: (0, 0))],
        out_specs=[pl.BlockSpec((tm, 512), row),
                   pl.BlockSpec((512, tm), col),
                   pl.BlockSpec((512, tm), col),
                   pl.BlockSpec((tm, 512), row),
                   pl.BlockSpec((tm, 512), row),
                   pl.BlockSpec((512, tm), col),
                   pl.BlockSpec((tm, 512), row),
                   pl.BlockSpec((tm, 512), row)],
        out_shape=[rows_bf, cols_bf, cols_bf, rows_bf, rows_bf, cols_bf, rows_f, rows_f],
        compiler_params=_params(("parallel",)),
        name="proj_gmlp_prompt",
    )(h, w_in_b, ln_g.reshape(1, -1), ln_b.reshape(1, -1), w_s, b_s_t, kbias)


def _lambda(lamp_ref, lam_init):
    lp = lamp_ref[...]
    d1 = jnp.sum(lp[0:1] * lp[1:2], axis=-1, keepdims=True)
    d2 = jnp.sum(lp[2:3] * lp[3:4], axis=-1, keepdims=True)
    return jnp.exp(d1) - jnp.exp(d2) + lam_init


def _head_slope(h):
    return jnp.exp2(-(8.0 / N_DIFF_HEADS) * (h + 1).astype(F32))


def _split_q(q):
    lane = lax.broadcasted_iota(jnp.int32, q.shape, 1)
    zero = jnp.zeros_like(q)
    return jnp.where(lane < DIFF_QK_DIM, q, zero), jnp.where(lane >= DIFF_QK_DIM, q, zero)


def _qk(qz, kc):
    return lax.dot_general(qz, kc, (((1,), (1,)), ((), ())), preferred_element_type=F32)


def _diff_post(o1, o2, lam, g, lam_init):
    o = o1 - lam * o2
    o = o * lax.rsqrt(jnp.mean(o * o, -1, keepdims=True) + RMS_EPS) * g
    return o * (1.0 - lam_init)


def _attn_prompt_kernel(qt1_ref, qt2_ref, ka1_ref, ka2_ref, vt_ref, diag_ref, lamp_ref, g_ref,
                        o_ref, m1, l1, a1, m2, l2, a2, *, tq, lam_init):
    h = pl.program_id(0)
    i = pl.program_id(1)
    chunk_bias = _head_slope(h) * (LOG2E * tq)
    maps = ((qt1_ref, ka1_ref, m1, l1, a1), (qt2_ref, ka2_ref, m2, l2, a2))
    for _, _, m, l, a in maps:
        m[...] = jnp.full(m.shape, -jnp.inf, F32)
        l[...] = jnp.zeros(l.shape, F32)
        a[...] = jnp.zeros(a.shape, F32)

    def step(j, diagonal):
        start = pl.multiple_of(j * tq, tq)
        vtc = vt_ref[:, pl.ds(start, tq)]
        c_j = chunk_bias * j.astype(F32)
        for qt_ref, ka_ref, m, l, a in maps:
            s = jnp.dot(ka_ref[pl.ds(start, tq), :], qt_ref[...],
                        preferred_element_type=F32)
            if diagonal:
                s = s + diag_ref[...]
            m_prev = m[...]
            m_new = jnp.maximum(m_prev, jnp.max(s, axis=0, keepdims=True) + c_j)
            alpha = jnp.exp2(m_prev - m_new)
            p = jnp.exp2(s - (m_new - c_j))
            l[...] = alpha * l[...] + jnp.sum(p, axis=0, keepdims=True)
            a[...] = alpha * a[...] + jnp.dot(vtc, p.astype(BF16), preferred_element_type=F32)
            m[...] = m_new

    def body(j, carry):
        step(j, False)
        return carry

    lax.fori_loop(0, i, body, 0)
    step(i, True)

    lam = _lambda(lamp_ref, lam_init)
    o = a1[...] / l1[...] - lam * (a2[...] / l2[...])
    o = o * lax.rsqrt(jnp.mean(o * o, axis=0, keepdims=True) + RMS_EPS) * g_ref[...]
    o_ref[...] = (o * (1.0 - lam_init)).T.astype(o_ref.dtype)


def _attn_prompt(qt1, qt2, ka1, ka2, vt, diag, lamp, g_col, lam_init, tq):
    s = ka1.shape[0]
    q_spec = pl.BlockSpec((DIFF_V_DIM, tq), lambda h, i: (h, i))
    k_spec = pl.BlockSpec((s, DIFF_V_DIM), lambda h, i: (0, h))
    return pl.pallas_call(
        functools.partial(_attn_prompt_kernel, tq=tq, lam_init=lam_init),
        grid=(N_DIFF_HEADS, s // tq),
        in_specs=[q_spec, q_spec, k_spec, k_spec,
                  pl.BlockSpec((DIFF_V_DIM, s), lambda h, i: (h, 0)),
                  pl.BlockSpec((None, tq, tq), lambda h, i: (h, 0, 0)),
                  pl.BlockSpec((4, DIFF_QK_DIM), lambda h, i: (0, 0)),
                  pl.BlockSpec((DIFF_V_DIM, 1), lambda h, i: (0, 0))],
        out_specs=pl.BlockSpec((tq, DIFF_V_DIM), lambda h, i: (i, h)),
        out_shape=jax.ShapeDtypeStruct((s, D_DIFF), BF16),
        scratch_shapes=[pltpu.VMEM((1, tq), F32), pltpu.VMEM((1, tq), F32),
                        pltpu.VMEM((DIFF_V_DIM, tq), F32)] * 2,
        compiler_params=_params(("parallel", "arbitrary")),
        name="attn_prompt",
    )(qt1, qt2, ka1, ka2, vt, diag, lamp, g_col)


def _bf16_terms(x, n):
    terms = []
    rem = x
    for _ in range(n):
        part = lax.bitcast_convert_type(
            lax.bitcast_convert_type(rem, jnp.uint32) & jnp.uint32(0xFFFF0000), F32)
        terms.append(part)
        rem = rem - part
    return terms


def _alibi_tables(tk):
    slopes = jnp.asarray([2.0 ** (-8.0 * (h + 1) / N_DIFF_HEADS) for h in range(N_DIFF_HEADS)], F32)
    pos = jnp.arange(tk, dtype=F32)
    f = (slopes[:, None] * LOG2E) * pos[None, :]
    terms = jnp.stack(_bf16_terms(f, N_BIAS), axis=-1)
    head = jnp.zeros((N_DIFF_HEADS, tk, DIFF_V_DIM), F32)
    head = head.at[:, :, DIFF_QK_DIM:DIFF_QK_DIM + N_BIAS].set(terms)
    head = head.at[:, :, :N_BIAS].set(terms)
    kbias = head.transpose(1, 0, 2).reshape(tk, N_DIFF_HEADS * DIFF_V_DIM)
    s_i = jnp.arange(tk)[:, None]
    t_i = jnp.arange(tk)[None, :]
    ahead = jnp.maximum(s_i - t_i, 0).astype(F32)
    corr = (-2.0 * LOG2E) * slopes[:, None, None] * ahead[None]
    diag = jnp.where(((s_i // CHUNK) <= (t_i // CHUNK))[None], corr, -jnp.inf)
    return kbias, diag


def _attn_sample_kernel(q_ref, kn_ref, vn_ref, ck_ref, cv_ref, lamp_ref, g_ref, o_ref,
                        *, t, past, lam_init):
    h = pl.program_id(1)
    slope = _head_slope(h)
    q1, q2 = _split_q(q_ref[...])
    kc = ck_ref[...].astype(BF16)
    vc = cv_ref[...].astype(BF16)
    kn = kn_ref[...]
    vn = vn_ref[...]
    t_pos = past + lax.broadcasted_iota(jnp.int32, (t, 1), 0)

    def bias(s_pos):
        dist = jnp.abs(t_pos - s_pos).astype(F32)
        vis = (s_pos // CHUNK) <= (t_pos // CHUNK)
        return jnp.where(vis, -slope * dist, -jnp.inf)

    bias_c = bias(lax.broadcasted_iota(jnp.int32, (1, past), 1))
    bias_n = bias(past + lax.broadcasted_iota(jnp.int32, (1, t), 1))

    def one_map(qz):
        sc = _qk(qz, kc) + bias_c
        sn = _qk(qz, kn) + bias_n
        m = jnp.maximum(jnp.max(sc, axis=-1, keepdims=True), jnp.max(sn, axis=-1, keepdims=True))
        pc = jnp.exp(sc - m)
        pn = jnp.exp(sn - m)
        l = jnp.sum(pc, axis=-1, keepdims=True) + jnp.sum(pn, axis=-1, keepdims=True)
        o = (jnp.dot(pc.astype(BF16), vc, preferred_element_type=F32)
             + jnp.dot(pn.astype(BF16), vn, preferred_element_type=F32))
        return o / l

    lam = _lambda(lamp_ref, lam_init)
    o_ref[...] = _diff_post(one_map(q1), one_map(q2), lam, g_ref[...],
                            lam_init).astype(o_ref.dtype)


def _attn_sample(q, kb, vb, cache_k, cache_v, layer, lamp, g, lam_init, nb, t):
    past = cache_k.shape[2]
    return pl.pallas_call(
        functools.partial(_attn_sample_kernel, t=t, past=past, lam_init=lam_init),
        grid=(nb, N_DIFF_HEADS),
        in_specs=[pl.BlockSpec((t, DIFF_V_DIM), lambda b, h: (b, h)),
                  pl.BlockSpec((t, DIFF_V_DIM), lambda b, h: (b, h)),
                  pl.BlockSpec((t, DIFF_V_DIM), lambda b, h: (b, h)),
                  pl.BlockSpec((None, None, past, DIFF_V_DIM), lambda b, h: (layer, b, 0, h)),
                  pl.BlockSpec((None, None, past, DIFF_V_DIM), lambda b, h: (layer, b, 0, h)),
                  pl.BlockSpec((4, DIFF_QK_DIM), lambda b, h: (0, 0)),
                  pl.BlockSpec((1, DIFF_V_DIM), lambda b, h: (0, 0))],
        out_specs=pl.BlockSpec((t, DIFF_V_DIM), lambda b, h: (b, h)),
        out_shape=jax.ShapeDtypeStruct((nb * t, D_DIFF), BF16),
        compiler_params=_params(("parallel", "parallel")),
        name="attn_sample",
    )(q, kb, vb, cache_k, cache_v, lamp, g)


def _split_bf16(x):
    hi = x.astype(BF16)
    lo = (x - hi.astype(F32)).astype(BF16)
    return hi, lo


def _top2_sum(a, b, c, d):
    return jnp.maximum(jnp.maximum(jnp.maximum(a + b, a + c), jnp.maximum(a + d, b + c)),
                       jnp.maximum(b + d, c + d))


def _gates_t(aff, sel):
    rows = [sel[e:e + 1, :] for e in range(N_EXPERTS)]
    arow = [aff[e:e + 1, :] for e in range(N_EXPERTS)]
    gs = [_top2_sum(*rows[4 * g:4 * g + 4]) for g in range(N_GROUPS)]
    best = jnp.maximum(jnp.maximum(gs[0], gs[1]), jnp.maximum(gs[2], gs[3]))
    is_g = []
    taken = None
    for g in range(N_GROUPS):
        hit = gs[g] == best
        if taken is None:
            is_g.append(hit)
            taken = hit
        else:
            is_g.append(hit & ~taken)
            taken = taken | hit
    vals, affs = [], []
    for i in range(EXPERTS_PER_GROUP):
        v = rows[i]
        a = arow[i]
        for g in range(1, N_GROUPS):
            v = jnp.where(is_g[g], rows[4 * g + i], v)
            a = jnp.where(is_g[g], arow[4 * g + i], a)
        vals.append(v)
        affs.append(a)
    picked = []
    for i in range(EXPERTS_PER_GROUP):
        rank = jnp.zeros_like(vals[i])
        for j in range(EXPERTS_PER_GROUP):
            if j == i:
                continue
            ahead = (vals[j] >= vals[i]) if j < i else (vals[j] > vals[i])
            rank = rank + jnp.where(ahead, 1.0, 0.0)
        picked.append(rank < 2.0)
    wsum = jnp.zeros_like(affs[0])
    for i in range(EXPERTS_PER_GROUP):
        wsum = wsum + jnp.where(picked[i], affs[i], 0.0)
    row_id = lax.broadcasted_iota(jnp.int32, aff.shape, 0)
    gates = jnp.zeros(aff.shape, F32)
    for g in range(N_GROUPS):
        for i in range(EXPERTS_PER_GROUP):
            w = jnp.where(is_g[g] & picked[i], affs[i] / wsum, 0.0)
            gates = jnp.where(row_id == 4 * g + i, w, gates)
    return gates


def _mix_kernel(h_ref, oa_ref, ob_ref, wo_ref, bo_ref, g_ref, b_ref, wr_hi_ref, wr_lo_ref,
                rb_ref, h1_ref, h1b_ref, gates_ref, *, tm):
    mix = (jnp.dot(oa_ref[...], wo_ref[:D_GMLP, :], preferred_element_type=F32)
           + jnp.dot(ob_ref[...], wo_ref[D_GMLP:, :], preferred_element_type=F32)
           + bo_ref[...])
    h1 = _ln(ALPHA * h_ref[...] + mix, g_ref[...], b_ref[...])
    h1_ref[...] = h1
    h1b_ref[...] = h1.astype(BF16)
    hi, lo = _split_bf16(h1)
    logits = (jnp.dot(hi, wr_hi_ref[...], preferred_element_type=F32)
              + jnp.dot(lo, wr_hi_ref[...], preferred_element_type=F32)
              + jnp.dot(hi, wr_lo_ref[...], preferred_element_type=F32))
    lt = logits.T[:N_EXPERTS, :]
    aff = 1.0 / (1.0 + jnp.exp(-lt))
    sel = aff + rb_ref[...]
    gates = _gates_t(aff, sel)
    gpad = jnp.concatenate([gates, jnp.zeros((LANES - N_EXPERTS, tm), F32)], axis=0)
    gates_ref[...] = gpad.T


def _mix(h, o_a, o_b, w_out_b, b_out, ln_g, ln_b, wr_hi, wr_lo, rb, tm):
    t = h.shape[0]
    row = lambda i: (i, 0)
    const2 = lambda i: (0, 0)
    return pl.pallas_call(
        functools.partial(_mix_kernel, tm=tm),
        grid=(t // tm,),
        in_specs=[pl.BlockSpec((tm, D_MODEL), row),
                  pl.BlockSpec((tm, D_GMLP), row),
                  pl.BlockSpec((tm, D_DIFF), row),
                  pl.BlockSpec((D_MODEL, D_MODEL), const2),
                  pl.BlockSpec((1, D_MODEL), const2),
                  pl.BlockSpec((1, D_MODEL), const2),
                  pl.BlockSpec((1, D_MODEL), const2),
                  pl.BlockSpec((D_MODEL, LANES), const2),
                  pl.BlockSpec((D_MODEL, LANES), const2),
                  pl.BlockSpec((N_EXPERTS, 1), const2)],
        out_specs=[pl.BlockSpec((tm, D_MODEL), row),
                   pl.BlockSpec((tm, D_MODEL), row),
                   pl.BlockSpec((tm, LANES), row)],
        out_shape=[jax.ShapeDtypeStruct((t, D_MODEL), F32),
                   jax.ShapeDtypeStruct((t, D_MODEL), BF16),
                   jax.ShapeDtypeStruct((t, LANES), F32)],
        compiler_params=_params(("parallel",)),
        name="mix_router",
    )(h, o_a, o_b, w_out_b, b_out.reshape(1, -1), ln_g.reshape(1, -1), ln_b.reshape(1, -1),
      wr_hi, wr_lo, rb)


def _gelu_tanh(x):
    return 0.5 * x * (1.0 + jnp.tanh(math.sqrt(2.0 / math.pi) * (x + 0.044715 * (x * x * x))))


def _moe_kernel(x_ref, gates_ref, h1_ref, wu_ref, bu_ref, wd_ref, bd_ref, g_ref, b_ref,
                o_ref, acc_ref):
    e = pl.program_id(1)

    @pl.when(e == 0)
    def _():
        acc_ref[...] = jnp.zeros(acc_ref.shape, F32)

    gates = gates_ref[...]
    lane = lax.broadcasted_iota(jnp.int32, gates.shape, 1)
    g_e = jnp.sum(jnp.where(lane == e, gates, 0.0), axis=-1, keepdims=True)
    up = jnp.dot(x_ref[...], wu_ref[...], preferred_element_type=F32) + bu_ref[...]
    hid = (_gelu_tanh(up) * g_e).astype(BF16)
    acc_ref[...] += jnp.dot(hid, wd_ref[...], preferred_element_type=F32)

    @pl.when(e == N_EXPERTS - 1)
    def _():
        f = acc_ref[...] + jnp.dot(gates.astype(BF16), bd_ref[...], preferred_element_type=F32)
        o_ref[...] = _ln(ALPHA * h1_ref[...] + f, g_ref[...], b_ref[...])


def _moe(h1, h1b, gates, w_up_b, b_up, w_down_b, b_down_pad, ln_g, ln_b, tm):
    t = h1.shape[0]
    row = lambda i, e: (i, 0)
    const2 = lambda i, e: (0, 0)
    return pl.pallas_call(
        _moe_kernel,
        grid=(t // tm, N_EXPERTS),
        in_specs=[pl.BlockSpec((tm, D_MODEL), row),
                  pl.BlockSpec((tm, LANES), row),
                  pl.BlockSpec((tm, D_MODEL), row),
                  pl.BlockSpec((None, D_MODEL, D_EXPERT), lambda i, e: (e, 0, 0)),
                  pl.BlockSpec((None, 1, D_EXPERT), lambda i, e: (e, 0, 0)),
                  pl.BlockSpec((None, D_EXPERT, D_MODEL), lambda i, e: (e, 0, 0)),
                  pl.BlockSpec((LANES, D_MODEL), const2),
                  pl.BlockSpec((1, D_MODEL), const2),
                  pl.BlockSpec((1, D_MODEL), const2)],
        out_specs=pl.BlockSpec((tm, D_MODEL), row),
        out_shape=jax.ShapeDtypeStruct((t, D_MODEL), F32),
        scratch_shapes=[pltpu.VMEM((tm, D_MODEL), F32)],
        compiler_params=_params(("parallel", "arbitrary")),
        name="moe_ln",
    )(h1b, gates, h1, w_up_b, b_up.reshape(N_EXPERTS, 1, D_EXPERT), w_down_b, b_down_pad,
      ln_g.reshape(1, -1), ln_b.reshape(1, -1))


def kernel(x_prompt, x_sample, cache_k, cache_v, ln_in_g, ln_in_b, w_in, w_out, b_out,
           gmlp_ln_g, gmlp_ln_b, gmlp_w_s, gmlp_b_s, lambda_q1, lambda_k1, lambda_q2,
           lambda_k2, subln_g, ln_mix_g, ln_mix_b, w_router, router_bias, w_up, b_up,
           w_down, b_down, ln_ffn_g, ln_ffn_b):
    nbp, seq, d = x_prompt.shape
    nbs, tdec, _ = x_sample.shape
    assert nbp == 1 and d == D_MODEL
    past = cache_k.shape[2]
    ts = nbs * tdec
    ck = cache_k.reshape(DEPTH, nbs, past, N_DIFF_HEADS * DIFF_V_DIM)
    cv = cache_v.reshape(DEPTH, nbs, past, N_DIFF_HEADS * DIFF_V_DIM)

    hp = _input_ln(x_prompt.reshape(seq, d), ln_in_g, ln_in_b, 1024)
    hs = _input_ln(x_sample.reshape(ts, d), ln_in_g, ln_in_b, ts)

    wr_pad = jnp.pad(w_router, ((0, 0), (0, LANES - N_EXPERTS)))
    wr_hi, wr_lo = (w.astype(BF16) for w in _bf16_terms(wr_pad, 2))
    rb = router_bias.astype(F32).reshape(N_EXPERTS, 1)
    kbias, diag = _alibi_tables(ATTN_BLOCK)

    kp, vp, ksm, vsm, gvs = [], [], [], [], []
    for l in range(DEPTH):
        lam_init = 0.8 - 0.6 * math.exp(-0.3 * l)
        w_in_b = w_in[l].astype(BF16)
        w_out_b = w_out[l].astype(BF16)
        w_up_b = w_up[l].astype(BF16)
        w_down_b = w_down[l].astype(BF16)
        b_down_pad = jnp.pad(b_down[l], ((0, LANES - N_EXPERTS), (0, 0))).astype(BF16)
        b_s_t = gmlp_b_s[l].T
        lamp = jnp.stack([lambda_q1[l], lambda_k1[l], lambda_q2[l], lambda_k2[l]]).astype(F32)

        def tail(h, o_a, o_b, tm_mix, tm_moe):
            h1, h1b, gates = _mix(h, o_a, o_b, w_out_b, b_out[l], ln_mix_g[l], ln_mix_b[l],
                                  wr_hi, wr_lo, rb, tm_mix)
            return _moe(h1, h1b, gates, w_up_b, b_up[l], w_down_b, b_down_pad,
                        ln_ffn_g[l], ln_ffn_b[l], tm_moe)

        o_a, qt1, qt2, ka1, ka2, vt, kf, vf = _proj_prompt(
            hp, w_in_b, gmlp_ln_g[l], gmlp_ln_b[l], gmlp_w_s[l], b_s_t, kbias, ATTN_BLOCK)
        o_b = _attn_prompt(qt1, qt2, ka1, ka2, vt, diag, lamp,
                           subln_g[l].reshape(DIFF_V_DIM, 1), lam_init, ATTN_BLOCK)
        hp = tail(hp, o_a, o_b, 512, 1024)
        kp.append(kf)
        vp.append(vf)

        o_a, q, kf, vf, kb, vb, vn = _proj_sample(hs, w_in_b, gmlp_ln_g[l], gmlp_ln_b[l],
                                                  gmlp_w_s[l], b_s_t, ts, tdec)
        o_b = _attn_sample(q, kb, vb, ck, cv, l, lamp, subln_g[l].reshape(1, DIFF_V_DIM),
                           lam_init, nbs, tdec)
        hs = tail(hs, o_a, o_b, ts, ts)
        ksm.append(kf)
        vsm.append(vf)
        gvs.append(vn)

    hd = (N_DIFF_HEADS, DIFF_V_DIM)
    return (hp.reshape(1, seq, d),
            hs.reshape(nbs, tdec, d),
            jnp.stack(kp).reshape(DEPTH, 1, seq, *hd),
            jnp.stack(vp).reshape(DEPTH, 1, seq, *hd),
            jnp.stack(ksm).reshape(DEPTH, nbs, tdec, *hd),
            jnp.stack(vsm).reshape(DEPTH, nbs, tdec, *hd),
            jnp.stack(gvs).reshape(DEPTH, nbs, tdec, D_GMLP))
```

```python
import functools
import math

import jax
import jax.numpy as jnp
from jax import lax
from jax.experimental import pallas as pl
from jax.experimental.pallas import tpu as pltpu

F32 = jnp.float32
BF16 = jnp.bfloat16

D_MODEL = 1024
DEPTH = 4
CHUNK = 64
D_GMLP = 512
GMLP_BLOCK = 128
GMLP_HEAD = 128
N_GMLP_HEADS = 4
D_DIFF = 512
N_DIFF_HEADS = 4
DIFF_V_DIM = 128
DIFF_QK_DIM = 64
OFF_Q = 2 * D_GMLP
OFF_K = OFF_Q + 512
OFF_V = OFF_K + 512
D_IN = OFF_V + D_DIFF
N_EXPERTS = 16
N_GROUPS = 4
EXPERTS_PER_GROUP = 4
D_EXPERT = 512
ALPHA = (2.0 * DEPTH) ** 0.25
LN_EPS = 1e-5
RMS_EPS = 1e-5
QK_SCALE = DIFF_QK_DIM ** -0.5
LOG2E = math.log2(math.e)
ATTN_BLOCK = 512
N_BIAS = 3
LANES = 128
VMEM_LIMIT = 48 * 1024 * 1024


def _ln(x, g, b):
    mu = jnp.mean(x, -1, keepdims=True)
    xc = x - mu
    var = jnp.mean(xc * xc, -1, keepdims=True)
    return xc * lax.rsqrt(var + LN_EPS) * g + b


def _params(sem):
    return pltpu.CompilerParams(dimension_semantics=sem, vmem_limit_bytes=VMEM_LIMIT)


def _ln_kernel(x_ref, g_ref, b_ref, o_ref):
    o_ref[...] = _ln(x_ref[...], g_ref[...], b_ref[...])


def _input_ln(x, g, b, tm):
    t, d = x.shape
    return pl.pallas_call(
        _ln_kernel,
        grid=(t // tm,),
        in_specs=[pl.BlockSpec((tm, d), lambda i: (i, 0)),
                  pl.BlockSpec((1, d), lambda i: (0, 0)),
                  pl.BlockSpec((1, d), lambda i: (0, 0))],
        out_specs=pl.BlockSpec((tm, d), lambda i: (i, 0)),
        out_shape=jax.ShapeDtypeStruct((t, d), F32),
        compiler_params=_params(("parallel",)),
        name="input_ln",
    )(x, g.reshape(1, d), b.reshape(1, d))


def _sgu(u, v_nb, ws_ref, bs_ref, oa_ref, tm, blk):
    ri = lax.broadcasted_iota(jnp.int32, (blk, blk), 0) // CHUNK
    ci = lax.broadcasted_iota(jnp.int32, (blk, blk), 1) // CHUNK
    vis = ci <= ri
    for g in range(N_GMLP_HEADS):
        w_g = jnp.where(vis, ws_ref[g, :blk, :blk], 0.0).astype(BF16)
        b_g = bs_ref[:blk, g:g + 1]
        c0 = g * GMLP_HEAD
        for r in range(tm // blk):
            r0 = r * blk
            s = jnp.dot(w_g, v_nb[r0:r0 + blk, c0:c0 + GMLP_HEAD],
                        preferred_element_type=F32) + b_g
            oa_ref[r0:r0 + blk, c0:c0 + GMLP_HEAD] = (
                u[r0:r0 + blk, c0:c0 + GMLP_HEAD] * s).astype(BF16)


def _proj_sample_kernel(h_ref, w_ref, lng_ref, lnb_ref, ws_ref, bs_ref,
                        oa_ref, q_ref, kf_ref, vf_ref, kb_ref, vb_ref, vn_ref, *, tm, blk):
    hb = h_ref[...].astype(BF16)

    def proj(off, width):
        return jnp.dot(hb, w_ref[:, off:off + width], preferred_element_type=F32)

    u = proj(0, D_GMLP)
    v_a = proj(D_GMLP, D_GMLP)
    q = proj(OFF_Q, 512)
    k = proj(OFF_K, 512)
    v = proj(OFF_V, D_DIFF)
    q_ref[...] = (q * QK_SCALE).astype(BF16)
    kf_ref[...] = k
    vf_ref[...] = v
    kb_ref[...] = k.astype(BF16)
    vb_ref[...] = v.astype(BF16)
    v_n = _ln(v_a, lng_ref[...], lnb_ref[...])
    vn_ref[...] = v_n
    _sgu(u, v_n.astype(BF16), ws_ref, bs_ref, oa_ref, tm, blk)


def _proj_prompt_kernel(h_ref, w_ref, lng_ref, lnb_ref, ws_ref, bs_ref, kbias_ref,
                        oa_ref, qt1_ref, qt2_ref, ka1_ref, ka2_ref, vt_ref, kf_ref, vf_ref,
                        *, tm, blk):
    hb = h_ref[...].astype(BF16)

    def proj(off, width):
        return jnp.dot(hb, w_ref[:, off:off + width], preferred_element_type=F32)

    u = proj(0, D_GMLP)
    v_a = proj(D_GMLP, D_GMLP)
    q = proj(OFF_Q, 512)
    k = proj(OFF_K, 512)
    v = proj(OFF_V, D_DIFF)
    for hd in range(N_DIFF_HEADS):
        lanes = slice(hd * DIFF_V_DIM, (hd + 1) * DIFF_V_DIM)
        kf_ref[:, hd, :] = k[:, lanes]
        vf_ref[:, hd, :] = v[:, lanes]
    vt_ref[...] = v.T.astype(BF16)

    lane = lax.broadcasted_iota(jnp.int32, k.shape, 1) % DIFF_V_DIM
    kbias = kbias_ref[...]
    ka1_ref[...] = jnp.where(lane < DIFF_QK_DIM, k, kbias).astype(BF16)
    ka2_ref[...] = jnp.where(lane >= DIFF_QK_DIM, k, kbias).astype(BF16)

    qt = (q * (QK_SCALE * LOG2E)).T
    row = lax.broadcasted_iota(jnp.int32, qt.shape, 0) % DIFF_V_DIM
    ones1 = jnp.where((row >= DIFF_QK_DIM) & (row < DIFF_QK_DIM + N_BIAS), 1.0, 0.0)
    ones2 = jnp.where(row < N_BIAS, 1.0, 0.0)
    qt1_ref[...] = jnp.where(row < DIFF_QK_DIM, qt, ones1).astype(BF16)
    qt2_ref[...] = jnp.where(row >= DIFF_QK_DIM, qt, ones2).astype(BF16)

    v_n = _ln(v_a, lng_ref[...], lnb_ref[...])
    _sgu(u, v_n.astype(BF16), ws_ref, bs_ref, oa_ref, tm, blk)


def _proj_in_specs(tm, layer):
    row = lambda i: (i, 0)
    per_layer = lambda i: (layer, 0, 0)
    return [pl.BlockSpec((tm, D_MODEL), row),
            pl.BlockSpec((None, D_MODEL, D_IN), per_layer),
            pl.BlockSpec((None, 1, D_GMLP), per_layer),
            pl.BlockSpec((None, 1, D_GMLP), per_layer),
            pl.BlockSpec((None, N_GMLP_HEADS, GMLP_BLOCK, GMLP_BLOCK), lambda i: (layer, 0, 0, 0)),
            pl.BlockSpec((None, GMLP_BLOCK, N_GMLP_HEADS), per_layer)]


def _proj_sample(h, w_in_b, ln_g, ln_b, w_s, b_s_t, layer, tm, blk):
    t = h.shape[0]
    row = lambda i: (i, 0)
    return pl.pallas_call(
        functools.partial(_proj_sample_kernel, tm=tm, blk=blk),
        grid=(t // tm,),
        in_specs=_proj_in_specs(tm, layer),
        out_specs=[pl.BlockSpec((tm, 512), row)] * 7,
        out_shape=[jax.ShapeDtypeStruct((t, 512), BF16),
                   jax.ShapeDtypeStruct((t, 512), BF16),
                   jax.ShapeDtypeStruct((t, 512), F32),
                   jax.ShapeDtypeStruct((t, 512), F32),
                   jax.ShapeDtypeStruct((t, 512), BF16),
                   jax.ShapeDtypeStruct((t, 512), BF16),
                   jax.ShapeDtypeStruct((t, 512), F32)],
        compiler_params=_params(("parallel",)),
        name="proj_gmlp_sample",
    )(h, w_in_b, ln_g, ln_b, w_s, b_s_t)


def _proj_prompt(h, w_in_b, ln_g, ln_b, w_s, b_s_t, kbias, layer, tm):
    t = h.shape[0]
    row = lambda i: (i, 0)
    col = lambda i: (0, i)
    rows_bf = jax.ShapeDtypeStruct((t, 512), BF16)
    cols_bf = jax.ShapeDtypeStruct((512, t), BF16)
    heads_f = jax.ShapeDtypeStruct((1, 1, t, N_DIFF_HEADS, DIFF_V_DIM), F32)
    heads_spec = pl.BlockSpec((None, None, tm, N_DIFF_HEADS, DIFF_V_DIM),
                              lambda i: (0, 0, i, 0, 0))
    return pl.pallas_call(
        functools.partial(_proj_prompt_kernel, tm=tm, blk=GMLP_BLOCK),
        grid=(t // tm,),
        in_specs=_proj_in_specs(tm, layer) + [pl.BlockSpec((tm, 512), lambda i: (0, 0))],
        out_specs=[pl.BlockSpec((tm, 512), row),
                   pl.BlockSpec((512, tm), col),
                   pl.BlockSpec((512, tm), col),
                   pl.BlockSpec((tm, 512), row),
                   pl.BlockSpec((tm, 512), row),
                   pl.BlockSpec((512, tm), col),
                   heads_spec,
                   heads_spec],
        out_shape=[rows_bf, cols_bf, cols_bf, rows_bf, rows_bf, cols_bf, heads_f, heads_f],
        compiler_params=_params(("parallel",)),
        name="proj_gmlp_prompt",
    )(h, w_in_b, ln_g, ln_b, w_s, b_s_t, kbias)


def _lambda(lamp_ref, lam_init):
    lp = lamp_ref[...]
    d1 = jnp.sum(lp[0:1] * lp[1:2], axis=-1, keepdims=True)
    d2 = jnp.sum(lp[2:3] * lp[3:4], axis=-1, keepdims=True)
    return jnp.exp(d1) - jnp.exp(d2) + lam_init


def _head_slope(h):
    return jnp.exp2(-(8.0 / N_DIFF_HEADS) * (h + 1).astype(F32))


def _split_q(q):
    lane = lax.broadcasted_iota(jnp.int32, q.shape, 1)
    zero = jnp.zeros_like(q)
    return jnp.where(lane < DIFF_QK_DIM, q, zero), jnp.where(lane >= DIFF_QK_DIM, q, zero)


def _qk(qz, kc):
    return lax.dot_general(qz, kc, (((1,), (1,)), ((), ())), preferred_element_type=F32)


def _diff_post(o1, o2, lam, g, lam_init):
    o = o1 - lam * o2
    o = o * lax.rsqrt(jnp.mean(o * o, -1, keepdims=True) + RMS_EPS) * g
    return o * (1.0 - lam_init)


class _MapState:
    def __init__(self, qt, ka, m, l, acc, s_bufs, p_bufs, shift, alpha_s, alpha_p):
        self.qt, self.ka, self.m, self.l, self.acc = qt, ka, m, l, acc
        self.s_bufs, self.p_bufs = s_bufs, p_bufs
        self.shift, self.alpha_s, self.alpha_p = shift, alpha_s, alpha_p


def _attn_prompt_kernel(qt1_ref, qt2_ref, ka1_ref, ka2_ref, vt_ref, diag_ref, lamp_ref, g_ref,
                        o_ref, *scratch, tq, lam_init):
    h = pl.program_id(0)
    i = pl.program_id(1)
    chunk_bias = _head_slope(h) * (LOG2E * tq)
    per_map = len(scratch) // 2
    maps = []
    for n, (qt, ka) in enumerate(((qt1_ref, ka1_ref), (qt2_ref, ka2_ref))):
        m, l, acc, s0, s1, p0, p1, shift, alpha_s, alpha_p = scratch[n * per_map:(n + 1) * per_map]
        maps.append(_MapState(qt, ka, m, l, acc, (s0, s1), (p0, p1), shift, alpha_s, alpha_p))
        m[...] = jnp.full(m.shape, -jnp.inf, F32)
        l[...] = jnp.zeros(l.shape, F32)
        acc[...] = jnp.zeros(acc.shape, F32)

    def chunk_of(idx):
        j = jnp.where(idx == 0, i, idx - 1)
        shift = jnp.where((idx == 0) | (j < i), chunk_bias * j.astype(F32), -jnp.inf)
        return jnp.minimum(j, i), shift

    def scores(idx, slot, diagonal=False):
        j, c_j = chunk_of(idx)
        start = pl.multiple_of(j * tq, tq)
        for st in maps:
            s = jnp.dot(st.ka[pl.ds(start, tq), :], st.qt[...],
                        preferred_element_type=F32)
            if diagonal:
                s = s + diag_ref[...]
            st.s_bufs[slot][...] = s
            m_prev = st.m[...]
            m_new = jnp.maximum(m_prev, jnp.max(s, axis=0, keepdims=True) + c_j)
            st.alpha_s[slot:slot + 1, :] = jnp.exp2(m_prev - m_new)
            st.shift[slot:slot + 1, :] = m_new - c_j
            st.m[...] = m_new

    def exponentials(slot):
        for st in maps:
            alpha = st.alpha_s[slot:slot + 1, :]
            p = jnp.exp2(st.s_bufs[slot][...] - st.shift[slot:slot + 1, :])
            st.l[...] = alpha * st.l[...] + jnp.sum(p, axis=0, keepdims=True)
            st.p_bufs[slot][...] = p.astype(BF16)
            st.alpha_p[slot:slot + 1, :] = alpha

    def values(idx, slot):
        j, _ = chunk_of(idx)
        start = pl.multiple_of(j * tq, tq)
        vtc = vt_ref[:, pl.ds(start, tq)]
        for st in maps:
            st.acc[...] = (st.alpha_p[slot:slot + 1, :] * st.acc[...]
                           + jnp.dot(vtc, st.p_bufs[slot][...], preferred_element_type=F32))

    scores(0, 0, diagonal=True)
    scores(1, 1)
    exponentials(0)

    def body(pair, carry):
        k = 2 * pair
        scores(k + 2, 0)
        exponentials(1)
        values(k, 0)
        scores(k + 3, 1)
        exponentials(0)
        values(k + 1, 1)
        return carry

    lax.fori_loop(0, (i + 2) // 2, body, 0)

    lam = _lambda(lamp_ref, lam_init)
    st1, st2 = maps
    o = st1.acc[...] / st1.l[...] - lam * (st2.acc[...] / st2.l[...])
    o = o * lax.rsqrt(jnp.mean(o * o, axis=0, keepdims=True) + RMS_EPS) * g_ref[...]
    o_ref[...] = (o * (1.0 - lam_init)).T.astype(o_ref.dtype)


def _attn_prompt(qt1, qt2, ka1, ka2, vt, diag, lamp, g_col, lam_init, tq):
    s = ka1.shape[0]
    q_spec = pl.BlockSpec((DIFF_V_DIM, tq), lambda h, i: (h, i))
    k_spec = pl.BlockSpec((s, DIFF_V_DIM), lambda h, i: (0, h))
    row = pltpu.VMEM((1, tq), F32)
    two_rows = pltpu.VMEM((2, tq), F32)
    score = pltpu.VMEM((tq, tq), F32)
    prob = pltpu.VMEM((tq, tq), BF16)
    map_scratch = [row, row, pltpu.VMEM((DIFF_V_DIM, tq), F32), score, score, prob, prob,
                   two_rows, two_rows, two_rows]
    return pl.pallas_call(
        functools.partial(_attn_prompt_kernel, tq=tq, lam_init=lam_init),
        grid=(N_DIFF_HEADS, s // tq),
        in_specs=[q_spec, q_spec, k_spec, k_spec,
                  pl.BlockSpec((DIFF_V_DIM, s), lambda h, i: (h, 0)),
                  pl.BlockSpec((None, tq, tq), lambda h, i: (h, 0, 0)),
                  pl.BlockSpec((4, DIFF_QK_DIM), lambda h, i: (0, 0)),
                  pl.BlockSpec((DIFF_V_DIM, 1), lambda h, i: (0, 0))],
        out_specs=pl.BlockSpec((tq, DIFF_V_DIM), lambda h, i: (i, h)),
        out_shape=jax.ShapeDtypeStruct((s, D_DIFF), BF16),
        scratch_shapes=map_scratch * 2,
        compiler_params=_params(("parallel", "arbitrary")),
        name="attn_prompt",
    )(qt1, qt2, ka1, ka2, vt, diag, lamp, g_col)


def _bf16_terms(x, n):
    terms = []
    rem = x
    for _ in range(n):
        part = lax.bitcast_convert_type(
            lax.bitcast_convert_type(rem, jnp.uint32) & jnp.uint32(0xFFFF0000), F32)
        terms.append(part)
        rem = rem - part
    return terms


def _alibi_tables(tk):
    slopes = jnp.asarray([2.0 ** (-8.0 * (h + 1) / N_DIFF_HEADS) for h in range(N_DIFF_HEADS)], F32)
    pos = jnp.arange(tk, dtype=F32)
    f = (slopes[:, None] * LOG2E) * pos[None, :]
    terms = jnp.stack(_bf16_terms(f, N_BIAS), axis=-1)
    head = jnp.zeros((N_DIFF_HEADS, tk, DIFF_V_DIM), F32)
    head = head.at[:, :, DIFF_QK_DIM:DIFF_QK_DIM + N_BIAS].set(terms)
    head = head.at[:, :, :N_BIAS].set(terms)
    kbias = head.transpose(1, 0, 2).reshape(tk, N_DIFF_HEADS * DIFF_V_DIM)
    s_i = jnp.arange(tk)[:, None]
    t_i = jnp.arange(tk)[None, :]
    ahead = jnp.maximum(s_i - t_i, 0).astype(F32)
    corr = (-2.0 * LOG2E) * slopes[:, None, None] * ahead[None]
    diag = jnp.where(((s_i // CHUNK) <= (t_i // CHUNK))[None], corr, -jnp.inf)
    return kbias, diag


def _attn_sample_kernel(q_ref, kn_ref, vn_ref, ck_ref, cv_ref, lamp_ref, g_ref, o_ref,
                        *, t, past, lam_init):
    t_pos = past + lax.broadcasted_iota(jnp.int32, (2 * t, 1), 0) % t

    def neg_dist(s_pos):
        vis = (s_pos // CHUNK) <= (t_pos // CHUNK)
        return jnp.where(vis, -jnp.abs(t_pos - s_pos).astype(F32), -jnp.inf)

    nd_c = neg_dist(lax.broadcasted_iota(jnp.int32, (1, past), 1))
    nd_n = neg_dist(past + lax.broadcasted_iota(jnp.int32, (1, t), 1))
    lam = _lambda(lamp_ref, lam_init)
    for hd in range(N_DIFF_HEADS):
        slope = 2.0 ** (-8.0 * (hd + 1) / N_DIFF_HEADS)
        lanes = slice(hd * DIFF_V_DIM, (hd + 1) * DIFF_V_DIM)
        qs = jnp.concatenate(_split_q(q_ref[:, lanes]), axis=0)
        head_rows = pl.ds(hd, past, stride=N_DIFF_HEADS)
        kc = ck_ref[head_rows, :].astype(BF16)
        vc = cv_ref[head_rows, :].astype(BF16)
        sc = _qk(qs, kc) + slope * nd_c
        sn = _qk(qs, kn_ref[:, lanes]) + slope * nd_n
        m = jnp.maximum(jnp.max(sc, axis=-1, keepdims=True), jnp.max(sn, axis=-1, keepdims=True))
        pc = jnp.exp(sc - m)
        pn = jnp.exp(sn - m)
        l = jnp.sum(pc, axis=-1, keepdims=True) + jnp.sum(pn, axis=-1, keepdims=True)
        o = (jnp.dot(pc.astype(BF16), vc, preferred_element_type=F32)
             + jnp.dot(pn.astype(BF16), vn_ref[:, lanes], preferred_element_type=F32)) / l
        o_ref[:, lanes] = _diff_post(o[:t], o[t:], lam, g_ref[...], lam_init).astype(o_ref.dtype)


def _attn_sample(q, kb, vb, cache_k, cache_v, layer, lamp, g, lam_init, nb, t):
    past = cache_k.shape[2] // N_DIFF_HEADS
    tok = pl.BlockSpec((t, D_DIFF), lambda b: (b, 0))
    cache = pl.BlockSpec((None, None, past * N_DIFF_HEADS, DIFF_V_DIM),
                         lambda b: (layer, b, 0, 0))
    return pl.pallas_call(
        functools.partial(_attn_sample_kernel, t=t, past=past, lam_init=lam_init),
        grid=(nb,),
        in_specs=[tok, tok, tok, cache, cache,
                  pl.BlockSpec((4, DIFF_QK_DIM), lambda b: (0, 0)),
                  pl.BlockSpec((1, DIFF_V_DIM), lambda b: (0, 0))],
        out_specs=tok,
        out_shape=jax.ShapeDtypeStruct((nb * t, D_DIFF), BF16),
        compiler_params=_params(("parallel",)),
        name="attn_sample",
    )(q, kb, vb, cache_k, cache_v, lamp, g)


def _split_bf16(x):
    hi = x.astype(BF16)
    lo = (x - hi.astype(F32)).astype(BF16)
    return hi, lo


def _top2_sum(a, b, c, d):
    return jnp.maximum(jnp.maximum(jnp.maximum(a + b, a + c), jnp.maximum(a + d, b + c)),
                       jnp.maximum(b + d, c + d))


def _gates_t(aff, sel):
    rows = [sel[e:e + 1, :] for e in range(N_EXPERTS)]
    arow = [aff[e:e + 1, :] for e in range(N_EXPERTS)]
    gs = [_top2_sum(*rows[4 * g:4 * g + 4]) for g in range(N_GROUPS)]
    best = jnp.maximum(jnp.maximum(gs[0], gs[1]), jnp.maximum(gs[2], gs[3]))
    is_g = []
    taken = None
    for g in range(N_GROUPS):
        hit = gs[g] == best
        if taken is None:
            is_g.append(hit)
            taken = hit
        else:
            is_g.append(hit & ~taken)
            taken = taken | hit
    vals, affs = [], []
    for i in range(EXPERTS_PER_GROUP):
        v = rows[i]
        a = arow[i]
        for g in range(1, N_GROUPS):
            v = jnp.where(is_g[g], rows[4 * g + i], v)
            a = jnp.where(is_g[g], arow[4 * g + i], a)
        vals.append(v)
        affs.append(a)
    picked = []
    for i in range(EXPERTS_PER_GROUP):
        rank = jnp.zeros_like(vals[i])
        for j in range(EXPERTS_PER_GROUP):
            if j == i:
                continue
            ahead = (vals[j] >= vals[i]) if j < i else (vals[j] > vals[i])
            rank = rank + jnp.where(ahead, 1.0, 0.0)
        picked.append(rank < 2.0)
    wsum = jnp.zeros_like(affs[0])
    for i in range(EXPERTS_PER_GROUP):
        wsum = wsum + jnp.where(picked[i], affs[i], 0.0)
    row_id = lax.broadcasted_iota(jnp.int32, aff.shape, 0)
    gates = jnp.zeros(aff.shape, F32)
    for g in range(N_GROUPS):
        for i in range(EXPERTS_PER_GROUP):
            w = jnp.where(is_g[g] & picked[i], affs[i] / wsum, 0.0)
            gates = jnp.where(row_id == 4 * g + i, w, gates)
    return gates


def _mix_kernel(h_ref, oa_ref, ob_ref, wo_ref, bo_ref, g_ref, b_ref, wr_hi_ref, wr_lo_ref,
                rb_ref, h1_ref, h1b_ref, gates_ref, *, tm):
    mix = (jnp.dot(oa_ref[...], wo_ref[:D_GMLP, :], preferred_element_type=F32)
           + jnp.dot(ob_ref[...], wo_ref[D_GMLP:, :], preferred_element_type=F32)
           + bo_ref[...])
    h1 = _ln(ALPHA * h_ref[...] + mix, g_ref[...], b_ref[...])
    h1_ref[...] = h1
    h1b_ref[...] = h1.astype(BF16)
    hi, lo = _split_bf16(h1)
    logits = (jnp.dot(hi, wr_hi_ref[...], preferred_element_type=F32)
              + jnp.dot(lo, wr_hi_ref[...], preferred_element_type=F32)
              + jnp.dot(hi, wr_lo_ref[...], preferred_element_type=F32))
    lt = logits.T[:N_EXPERTS, :]
    aff = 1.0 / (1.0 + jnp.exp(-lt))
    sel = aff + rb_ref[...]
    gates = _gates_t(aff, sel)
    gpad = jnp.concatenate([gates, jnp.zeros((LANES - N_EXPERTS, tm), F32)], axis=0)
    gates_ref[...] = gpad.T


def _mix(h, o_a, o_b, w_out_b, layer, b_out, ln_g, ln_b, wr_hi, wr_lo, rb, tm):
    t = h.shape[0]
    row = lambda i: (i, 0)
    const2 = lambda i: (0, 0)
    return pl.pallas_call(
        functools.partial(_mix_kernel, tm=tm),
        grid=(t // tm,),
        in_specs=[pl.BlockSpec((tm, D_MODEL), row),
                  pl.BlockSpec((tm, D_GMLP), row),
                  pl.BlockSpec((tm, D_DIFF), row),
                  pl.BlockSpec((None, D_MODEL, D_MODEL), lambda i: (layer, 0, 0)),
                  pl.BlockSpec((1, D_MODEL), const2),
                  pl.BlockSpec((1, D_MODEL), const2),
                  pl.BlockSpec((1, D_MODEL), const2),
                  pl.BlockSpec((D_MODEL, LANES), const2),
                  pl.BlockSpec((D_MODEL, LANES), const2),
                  pl.BlockSpec((N_EXPERTS, 1), const2)],
        out_specs=[pl.BlockSpec((tm, D_MODEL), row),
                   pl.BlockSpec((tm, D_MODEL), row),
                   pl.BlockSpec((tm, LANES), row)],
        out_shape=[jax.ShapeDtypeStruct((t, D_MODEL), F32),
                   jax.ShapeDtypeStruct((t, D_MODEL), BF16),
                   jax.ShapeDtypeStruct((t, LANES), F32)],
        compiler_params=_params(("parallel",)),
        name="mix_router",
    )(h, o_a, o_b, w_out_b, b_out.reshape(1, -1), ln_g.reshape(1, -1), ln_b.reshape(1, -1),
      wr_hi, wr_lo, rb)


def _gelu_tanh(x):
    return 0.5 * x * (1.0 + jnp.tanh(math.sqrt(2.0 / math.pi) * (x + 0.044715 * (x * x * x))))


def _moe_kernel(x_ref, gates_ref, h1_ref, wu_ref, bu_ref, wd_ref, bd_ref, g_ref, b_ref,
                o_ref, acc_ref):
    e = pl.program_id(1)

    @pl.when(e == 0)
    def _():
        acc_ref[...] = jnp.zeros(acc_ref.shape, F32)

    gates = gates_ref[...]
    lane = lax.broadcasted_iota(jnp.int32, gates.shape, 1)
    g_e = jnp.sum(jnp.where(lane == e, gates, 0.0), axis=-1, keepdims=True)
    up = jnp.dot(x_ref[...], wu_ref[...], preferred_element_type=F32) + bu_ref[...]
    hid = (_gelu_tanh(up) * g_e).astype(BF16)
    acc_ref[...] += jnp.dot(hid, wd_ref[...], preferred_element_type=F32)

    @pl.when(e == N_EXPERTS - 1)
    def _():
        f = acc_ref[...] + jnp.dot(gates.astype(BF16), bd_ref[...], preferred_element_type=F32)
        o_ref[...] = _ln(ALPHA * h1_ref[...] + f, g_ref[...], b_ref[...])


def _moe(h1, h1b, gates, w_up_b, b_up, w_down_b, b_down_pad, layer, ln_g, ln_b, tm):
    t = h1.shape[0]
    row = lambda i, e: (i, 0)
    const2 = lambda i, e: (0, 0)
    return pl.pallas_call(
        _moe_kernel,
        grid=(t // tm, N_EXPERTS),
        in_specs=[pl.BlockSpec((tm, D_MODEL), row),
                  pl.BlockSpec((tm, LANES), row),
                  pl.BlockSpec((tm, D_MODEL), row),
                  pl.BlockSpec((None, None, D_MODEL, D_EXPERT), lambda i, e: (layer, e, 0, 0)),
                  pl.BlockSpec((None, None, 1, D_EXPERT), lambda i, e: (layer, e, 0, 0)),
                  pl.BlockSpec((None, None, D_EXPERT, D_MODEL), lambda i, e: (layer, e, 0, 0)),
                  pl.BlockSpec((None, LANES, D_MODEL), lambda i, e: (layer, 0, 0)),
                  pl.BlockSpec((1, D_MODEL), const2),
                  pl.BlockSpec((1, D_MODEL), const2)],
        out_specs=pl.BlockSpec((tm, D_MODEL), row),
        out_shape=jax.ShapeDtypeStruct((t, D_MODEL), F32),
        scratch_shapes=[pltpu.VMEM((tm, D_MODEL), F32)],
        compiler_params=_params(("parallel", "arbitrary")),
        name="moe_ln",
    )(h1b, gates, h1, w_up_b, b_up, w_down_b, b_down_pad, ln_g.reshape(1, -1), ln_b.reshape(1, -1))


def kernel(x_prompt, x_sample, cache_k, cache_v, ln_in_g, ln_in_b, w_in, w_out, b_out,
           gmlp_ln_g, gmlp_ln_b, gmlp_w_s, gmlp_b_s, lambda_q1, lambda_k1, lambda_q2,
           lambda_k2, subln_g, ln_mix_g, ln_mix_b, w_router, router_bias, w_up, b_up,
           w_down, b_down, ln_ffn_g, ln_ffn_b):
    nbp, seq, d = x_prompt.shape
    nbs, tdec, _ = x_sample.shape
    assert nbp == 1 and d == D_MODEL
    ts = nbs * tdec
    past = cache_k.shape[2]
    ck_rows = cache_k.reshape(DEPTH, nbs, past * N_DIFF_HEADS, DIFF_V_DIM)
    cv_rows = cache_v.reshape(DEPTH, nbs, past * N_DIFF_HEADS, DIFF_V_DIM)

    hp = _input_ln(x_prompt.reshape(seq, d), ln_in_g, ln_in_b, 1024)
    hs = _input_ln(x_sample.reshape(ts, d), ln_in_g, ln_in_b, ts)

    wr_pad = jnp.pad(w_router, ((0, 0), (0, LANES - N_EXPERTS)))
    wr_hi, wr_lo = (w.astype(BF16) for w in _bf16_terms(wr_pad, 2))
    rb = router_bias.astype(F32).reshape(N_EXPERTS, 1)
    kbias, diag = _alibi_tables(ATTN_BLOCK)

    w_in_b = w_in.astype(BF16)
    w_out_b = w_out.astype(BF16)
    w_up_b = w_up.astype(BF16)
    w_down_b = w_down.astype(BF16)
    b_up4 = b_up.reshape(DEPTH, N_EXPERTS, 1, D_EXPERT)
    b_down_pad = jnp.pad(b_down, ((0, 0), (0, LANES - N_EXPERTS), (0, 0))).astype(BF16)
    gln_g = gmlp_ln_g.reshape(DEPTH, 1, D_GMLP)
    gln_b = gmlp_ln_b.reshape(DEPTH, 1, D_GMLP)
    b_s_t = gmlp_b_s.transpose(0, 2, 1)

    kp, vp, ksm, vsm, gvs = [], [], [], [], []
    for l in range(DEPTH):
        lam_init = 0.8 - 0.6 * math.exp(-0.3 * l)
        lamp = jnp.stack([lambda_q1[l], lambda_k1[l], lambda_q2[l], lambda_k2[l]]).astype(F32)

        def tail(h, o_a, o_b, tm_mix, tm_moe):
            h1, h1b, gates = _mix(h, o_a, o_b, w_out_b, l, b_out[l], ln_mix_g[l], ln_mix_b[l],
                                  wr_hi, wr_lo, rb, tm_mix)
            return _moe(h1, h1b, gates, w_up_b, b_up4, w_down_b, b_down_pad, l,
                        ln_ffn_g[l], ln_ffn_b[l], tm_moe)

        o_a, qt1, qt2, ka1, ka2, vt, kf, vf = _proj_prompt(
            hp, w_in_b, gln_g, gln_b, gmlp_w_s, b_s_t, kbias, l, ATTN_BLOCK)
        o_b = _attn_prompt(qt1, qt2, ka1, ka2, vt, diag, lamp,
                           subln_g[l].reshape(DIFF_V_DIM, 1), lam_init, ATTN_BLOCK)
        hp = tail(hp, o_a, o_b, 512, 1024)
        kp.append(kf)
        vp.append(vf)

        o_a, q, kf, vf, kb, vb, vn = _proj_sample(hs, w_in_b, gln_g, gln_b, gmlp_w_s, b_s_t,
                                                  l, ts, tdec)
        o_b = _attn_sample(q, kb, vb, ck_rows, cv_rows, l, lamp,
                           subln_g[l].reshape(1, DIFF_V_DIM), lam_init, nbs, tdec)
        hs = tail(hs, o_a, o_b, ts, ts)
        ksm.append(kf)
        vsm.append(vf)
        gvs.append(vn)

    hd = (N_DIFF_HEADS, DIFF_V_DIM)
    return (hp.reshape(1, seq, d),
            hs.reshape(nbs, tdec, d),
            jnp.concatenate(kp, axis=0),
            jnp.concatenate(vp, axis=0),
            jnp.stack(ksm).reshape(DEPTH, nbs, tdec, *hd),
            jnp.stack(vsm).reshape(DEPTH, nbs, tdec, *hd),
            jnp.stack(gvs).reshape(DEPTH, nbs, tdec, D_GMLP))
```

```python
import functools
import math

import jax
import jax.numpy as jnp
from jax import lax
from jax.experimental import pallas as pl
from jax.experimental.pallas import tpu as pltpu

F32 = jnp.float32
BF16 = jnp.bfloat16

D_MODEL = 1024
DEPTH = 4
CHUNK = 64
D_GMLP = 512
GMLP_BLOCK = 128
GMLP_HEAD = 128
N_GMLP_HEADS = 4
D_DIFF = 512
N_DIFF_HEADS = 4
DIFF_V_DIM = 128
DIFF_QK_DIM = 64
OFF_Q = 2 * D_GMLP
OFF_K = OFF_Q + 512
OFF_V = OFF_K + 512
D_IN = OFF_V + D_DIFF
N_EXPERTS = 16
N_GROUPS = 4
EXPERTS_PER_GROUP = 4
D_EXPERT = 512
ALPHA = (2.0 * DEPTH) ** 0.25
LN_EPS = 1e-5
RMS_EPS = 1e-5
QK_SCALE = DIFF_QK_DIM ** -0.5
LOG2E = math.log2(math.e)
ATTN_TK = 512
ATTN_TQ = 1024
N_BIAS = 3
LANES = 128
VMEM_LIMIT = 48 * 1024 * 1024


def _ln(x, g, b):
    mu = jnp.mean(x, -1, keepdims=True)
    xc = x - mu
    var = jnp.mean(xc * xc, -1, keepdims=True)
    return xc * lax.rsqrt(var + LN_EPS) * g + b


def _params(sem):
    return pltpu.CompilerParams(dimension_semantics=sem, vmem_limit_bytes=VMEM_LIMIT)


def _ln_kernel(x_ref, g_ref, b_ref, o_ref):
    o_ref[...] = _ln(x_ref[...], g_ref[...], b_ref[...])


def _input_ln(x, g, b, tm):
    t, d = x.shape
    return pl.pallas_call(
        _ln_kernel,
        grid=(t // tm,),
        in_specs=[pl.BlockSpec((tm, d), lambda i: (i, 0)),
                  pl.BlockSpec((1, d), lambda i: (0, 0)),
                  pl.BlockSpec((1, d), lambda i: (0, 0))],
        out_specs=pl.BlockSpec((tm, d), lambda i: (i, 0)),
        out_shape=jax.ShapeDtypeStruct((t, d), F32),
        compiler_params=_params(("parallel",)),
        name="input_ln",
    )(x, g.reshape(1, d), b.reshape(1, d))


def _sgu(u, v_nb, ws_ref, bs_ref, oa_ref, tm, blk):
    ri = lax.broadcasted_iota(jnp.int32, (blk, blk), 0) // CHUNK
    ci = lax.broadcasted_iota(jnp.int32, (blk, blk), 1) // CHUNK
    vis = ci <= ri
    for g in range(N_GMLP_HEADS):
        w_g = jnp.where(vis, ws_ref[g, :blk, :blk], 0.0).astype(BF16)
        b_g = bs_ref[:blk, g:g + 1]
        c0 = g * GMLP_HEAD
        for r in range(tm // blk):
            r0 = r * blk
            s = jnp.dot(w_g, v_nb[r0:r0 + blk, c0:c0 + GMLP_HEAD],
                        preferred_element_type=F32) + b_g
            oa_ref[r0:r0 + blk, c0:c0 + GMLP_HEAD] = (
                u[r0:r0 + blk, c0:c0 + GMLP_HEAD] * s).astype(BF16)


def _proj_sample_kernel(h_ref, w_ref, lng_ref, lnb_ref, ws_ref, bs_ref,
                        oa_ref, q_ref, kf_ref, vf_ref, kb_ref, vb_ref, vn_ref, *, tm, blk):
    hb = h_ref[...].astype(BF16)

    def proj(off, width):
        return jnp.dot(hb, w_ref[:, off:off + width], preferred_element_type=F32)

    u = proj(0, D_GMLP)
    v_a = proj(D_GMLP, D_GMLP)
    q = proj(OFF_Q, 512)
    k = proj(OFF_K, 512)
    v = proj(OFF_V, D_DIFF)
    q_ref[...] = (q * QK_SCALE).astype(BF16)
    kf_ref[...] = k
    vf_ref[...] = v
    kb_ref[...] = k.astype(BF16)
    vb_ref[...] = v.astype(BF16)
    v_n = _ln(v_a, lng_ref[...], lnb_ref[...])
    vn_ref[...] = v_n
    _sgu(u, v_n.astype(BF16), ws_ref, bs_ref, oa_ref, tm, blk)


def _proj_prompt_kernel(h_ref, w_ref, lng_ref, lnb_ref, ws_ref, bs_ref, kbias_ref,
                        oa_ref, qt1_ref, qt2_ref, ka1_ref, ka2_ref, vt_ref, kf_ref, vf_ref,
                        *, tm, blk):
    hb = h_ref[...].astype(BF16)

    def proj(off, width):
        return jnp.dot(hb, w_ref[:, off:off + width], preferred_element_type=F32)

    u = proj(0, D_GMLP)
    v_a = proj(D_GMLP, D_GMLP)
    q = proj(OFF_Q, 512)
    k = proj(OFF_K, 512)
    v = proj(OFF_V, D_DIFF)
    for hd in range(N_DIFF_HEADS):
        lanes = slice(hd * DIFF_V_DIM, (hd + 1) * DIFF_V_DIM)
        kf_ref[:, hd, :] = k[:, lanes]
        vf_ref[:, hd, :] = v[:, lanes]
    vt_ref[...] = v.T.astype(BF16)

    lane = lax.broadcasted_iota(jnp.int32, k.shape, 1) % DIFF_V_DIM
    kbias = kbias_ref[...]
    ka1_ref[...] = jnp.where(lane < DIFF_QK_DIM, k, kbias).astype(BF16)
    ka2_ref[...] = jnp.where(lane >= DIFF_QK_DIM, k, kbias).astype(BF16)

    qt = (q * (QK_SCALE * LOG2E)).T
    row = lax.broadcasted_iota(jnp.int32, qt.shape, 0) % DIFF_V_DIM
    ones1 = jnp.where((row >= DIFF_QK_DIM) & (row < DIFF_QK_DIM + N_BIAS), 1.0, 0.0)
    ones2 = jnp.where(row < N_BIAS, 1.0, 0.0)
    qt1_ref[...] = jnp.where(row < DIFF_QK_DIM, qt, ones1).astype(BF16)
    qt2_ref[...] = jnp.where(row >= DIFF_QK_DIM, qt, ones2).astype(BF16)

    v_n = _ln(v_a, lng_ref[...], lnb_ref[...])
    _sgu(u, v_n.astype(BF16), ws_ref, bs_ref, oa_ref, tm, blk)


def _proj_in_specs(tm, layer):
    row = lambda i: (i, 0)
    per_layer = lambda i: (layer, 0, 0)
    return [pl.BlockSpec((tm, D_MODEL), row),
            pl.BlockSpec((None, D_MODEL, D_IN), per_layer),
            pl.BlockSpec((None, 1, D_GMLP), per_layer),
            pl.BlockSpec((None, 1, D_GMLP), per_layer),
            pl.BlockSpec((None, N_GMLP_HEADS, GMLP_BLOCK, GMLP_BLOCK), lambda i: (layer, 0, 0, 0)),
            pl.BlockSpec((None, GMLP_BLOCK, N_GMLP_HEADS), per_layer)]


def _proj_sample(h, w_in_b, ln_g, ln_b, w_s, b_s_t, layer, tm, blk):
    t = h.shape[0]
    row = lambda i: (i, 0)
    return pl.pallas_call(
        functools.partial(_proj_sample_kernel, tm=tm, blk=blk),
        grid=(t // tm,),
        in_specs=_proj_in_specs(tm, layer),
        out_specs=[pl.BlockSpec((tm, 512), row)] * 7,
        out_shape=[jax.ShapeDtypeStruct((t, 512), BF16),
                   jax.ShapeDtypeStruct((t, 512), BF16),
                   jax.ShapeDtypeStruct((t, 512), F32),
                   jax.ShapeDtypeStruct((t, 512), F32),
                   jax.ShapeDtypeStruct((t, 512), BF16),
                   jax.ShapeDtypeStruct((t, 512), BF16),
                   jax.ShapeDtypeStruct((t, 512), F32)],
        compiler_params=_params(("parallel",)),
        name="proj_gmlp_sample",
    )(h, w_in_b, ln_g, ln_b, w_s, b_s_t)


def _proj_prompt(h, w_in_b, ln_g, ln_b, w_s, b_s_t, kbias, layer, tm):
    t = h.shape[0]
    row = lambda i: (i, 0)
    col = lambda i: (0, i)
    rows_bf = jax.ShapeDtypeStruct((t, 512), BF16)
    cols_bf = jax.ShapeDtypeStruct((512, t), BF16)
    heads_f = jax.ShapeDtypeStruct((1, 1, t, N_DIFF_HEADS, DIFF_V_DIM), F32)
    heads_spec = pl.BlockSpec((None, None, tm, N_DIFF_HEADS, DIFF_V_DIM),
                              lambda i: (0, 0, i, 0, 0))
    return pl.pallas_call(
        functools.partial(_proj_prompt_kernel, tm=tm, blk=GMLP_BLOCK),
        grid=(t // tm,),
        in_specs=_proj_in_specs(tm, layer) + [pl.BlockSpec((tm, 512), lambda i: (0, 0))],
        out_specs=[pl.BlockSpec((tm, 512), row),
                   pl.BlockSpec((512, tm), col),
                   pl.BlockSpec((512, tm), col),
                   pl.BlockSpec((tm, 512), row),
                   pl.BlockSpec((tm, 512), row),
                   pl.BlockSpec((512, tm), col),
                   heads_spec,
                   heads_spec],
        out_shape=[rows_bf, cols_bf, cols_bf, rows_bf, rows_bf, cols_bf, heads_f, heads_f],
        compiler_params=_params(("parallel",)),
        name="proj_gmlp_prompt",
    )(h, w_in_b, ln_g, ln_b, w_s, b_s_t, kbias)


def _lambda(lamp_ref, lam_init):
    lp = lamp_ref[...]
    d1 = jnp.sum(lp[0:1] * lp[1:2], axis=-1, keepdims=True)
    d2 = jnp.sum(lp[2:3] * lp[3:4], axis=-1, keepdims=True)
    return jnp.exp(d1) - jnp.exp(d2) + lam_init


def _head_slope(h):
    return jnp.exp2(-(8.0 / N_DIFF_HEADS) * (h + 1).astype(F32))


def _split_q(q):
    lane = lax.broadcasted_iota(jnp.int32, q.shape, 1)
    zero = jnp.zeros_like(q)
    return jnp.where(lane < DIFF_QK_DIM, q, zero), jnp.where(lane >= DIFF_QK_DIM, q, zero)


def _qk(qz, kc):
    return lax.dot_general(qz, kc, (((1,), (1,)), ((), ())), preferred_element_type=F32)


def _diff_post(o1, o2, lam, g, lam_init):
    o = o1 - lam * o2
    o = o * lax.rsqrt(jnp.mean(o * o, -1, keepdims=True) + RMS_EPS) * g
    return o * (1.0 - lam_init)


class _MapState:
    def __init__(self, qt, ka, m, l, acc, s_bufs, p_bufs, shift, alpha_s, alpha_p):
        self.qt, self.ka, self.m, self.l, self.acc = qt, ka, m, l, acc
        self.s_bufs, self.p_bufs = s_bufs, p_bufs
        self.shift, self.alpha_s, self.alpha_p = shift, alpha_s, alpha_p


def _attn_prompt_kernel(qt1_ref, qt2_ref, ka1_ref, ka2_ref, vt_ref, diag_ref, lamp_ref, g_ref,
                        o_ref, *scratch, tq, tk, lam_init):
    h = pl.program_id(0)
    i = pl.program_id(1)
    chunk_bias = _head_slope(h) * (LOG2E * tk)
    n_diag = tq // tk
    first_diag = i * n_diag
    per_map = len(scratch) // 2
    maps = []
    for n, (qt, ka) in enumerate(((qt1_ref, ka1_ref), (qt2_ref, ka2_ref))):
        m, l, acc, s0, s1, p0, p1, shift, alpha_s, alpha_p = scratch[n * per_map:(n + 1) * per_map]
        maps.append(_MapState(qt, ka, m, l, acc, (s0, s1), (p0, p1), shift, alpha_s, alpha_p))
        m[...] = jnp.full(m.shape, -jnp.inf, F32)
        l[...] = jnp.zeros(l.shape, F32)
        acc[...] = jnp.zeros(acc.shape, F32)

    def chunk_of(idx):
        j = jnp.where(idx < n_diag, first_diag + idx, idx - n_diag)
        return j, chunk_bias * j.astype(F32)

    def scores(idx, slot, diagonal=None):
        j, c_j = chunk_of(idx)
        start = pl.multiple_of(j * tk, tk)
        for st in maps:
            s = jnp.dot(st.ka[pl.ds(start, tk), :], st.qt[...],
                        preferred_element_type=F32)
            if diagonal is not None:
                s = s + diag_ref[diagonal * tk:(diagonal + 1) * tk, :]
            st.s_bufs[slot][...] = s
            m_prev = st.m[...]
            m_new = jnp.maximum(m_prev, jnp.max(s, axis=0, keepdims=True) + c_j)
            st.alpha_s[slot:slot + 1, :] = jnp.exp2(m_prev - m_new)
            st.shift[slot:slot + 1, :] = m_new - c_j
            st.m[...] = m_new

    def exponentials(slot):
        for st in maps:
            alpha = st.alpha_s[slot:slot + 1, :]
            p = jnp.exp2(st.s_bufs[slot][...] - st.shift[slot:slot + 1, :])
            st.l[...] = alpha * st.l[...] + jnp.sum(p, axis=0, keepdims=True)
            st.p_bufs[slot][...] = p.astype(BF16)
            st.alpha_p[slot:slot + 1, :] = alpha

    def values(idx, slot):
        j, _ = chunk_of(idx)
        start = pl.multiple_of(j * tk, tk)
        vtc = vt_ref[:, pl.ds(start, tk)]
        for st in maps:
            st.acc[...] = (st.alpha_p[slot:slot + 1, :] * st.acc[...]
                           + jnp.dot(vtc, st.p_bufs[slot][...], preferred_element_type=F32))

    scores(0, 0, diagonal=0)
    scores(1, 1, diagonal=1)
    exponentials(0)

    def body(pair, carry):
        k = 2 * pair
        scores(k + 2, 0)
        exponentials(1)
        values(k, 0)
        scores(k + 3, 1)
        exponentials(0)
        values(k + 1, 1)
        return carry

    lax.fori_loop(0, i, body, 0)
    last = 2 * i
    exponentials(1)
    values(last, 0)
    values(last + 1, 1)

    lam = _lambda(lamp_ref, lam_init)
    st1, st2 = maps
    o = st1.acc[...] / st1.l[...] - lam * (st2.acc[...] / st2.l[...])
    o = o * lax.rsqrt(jnp.mean(o * o, axis=0, keepdims=True) + RMS_EPS) * g_ref[...]
    o_ref[...] = (o * (1.0 - lam_init)).T.astype(o_ref.dtype)


def _attn_prompt(qt1, qt2, ka1, ka2, vt, diag, lamp, g_col, lam_init, tq, tk):
    s = ka1.shape[0]
    assert tq == 2 * tk
    once = pl.Buffered(1)
    q_spec = pl.BlockSpec((DIFF_V_DIM, tq), lambda h, i: (h, i))
    k_spec = pl.BlockSpec((s, DIFF_V_DIM), lambda h, i: (0, h), pipeline_mode=once)
    row = pltpu.VMEM((1, tq), F32)
    two_rows = pltpu.VMEM((2, tq), F32)
    score = pltpu.VMEM((tk, tq), F32)
    prob = pltpu.VMEM((tk, tq), BF16)
    map_scratch = [row, row, pltpu.VMEM((DIFF_V_DIM, tq), F32), score, score, prob, prob,
                   two_rows, two_rows, two_rows]
    return pl.pallas_call(
        functools.partial(_attn_prompt_kernel, tq=tq, tk=tk, lam_init=lam_init),
        grid=(N_DIFF_HEADS, s // tq),
        in_specs=[q_spec, q_spec, k_spec, k_spec,
                  pl.BlockSpec((DIFF_V_DIM, s), lambda h, i: (h, 0), pipeline_mode=once),
                  pl.BlockSpec((None, tq, tq), lambda h, i: (h, 0, 0), pipeline_mode=once),
                  pl.BlockSpec((4, DIFF_QK_DIM), lambda h, i: (0, 0)),
                  pl.BlockSpec((DIFF_V_DIM, 1), lambda h, i: (0, 0))],
        out_specs=pl.BlockSpec((tq, DIFF_V_DIM), lambda h, i: (i, h)),
        out_shape=jax.ShapeDtypeStruct((s, D_DIFF), BF16),
        scratch_shapes=map_scratch * 2,
        compiler_params=_params(("parallel", "arbitrary")),
        name="attn_prompt",
    )(qt1, qt2, ka1, ka2, vt, diag, lamp, g_col)


def _bf16_terms(x, n):
    terms = []
    rem = x
    for _ in range(n):
        part = lax.bitcast_convert_type(
            lax.bitcast_convert_type(rem, jnp.uint32) & jnp.uint32(0xFFFF0000), F32)
        terms.append(part)
        rem = rem - part
    return terms


def _alibi_tables(tk, tq):
    slopes = jnp.asarray([2.0 ** (-8.0 * (h + 1) / N_DIFF_HEADS) for h in range(N_DIFF_HEADS)], F32)
    pos = jnp.arange(tk, dtype=F32)
    f = (slopes[:, None] * LOG2E) * pos[None, :]
    terms = jnp.stack(_bf16_terms(f, N_BIAS), axis=-1)
    head = jnp.zeros((N_DIFF_HEADS, tk, DIFF_V_DIM), F32)
    head = head.at[:, :, DIFF_QK_DIM:DIFF_QK_DIM + N_BIAS].set(terms)
    head = head.at[:, :, :N_BIAS].set(terms)
    kbias = head.transpose(1, 0, 2).reshape(tk, N_DIFF_HEADS * DIFF_V_DIM)
    s_i = jnp.arange(tq)[:, None]
    t_i = jnp.arange(tq)[None, :]
    ahead = jnp.maximum(s_i - t_i, 0).astype(F32)
    corr = (-2.0 * LOG2E) * slopes[:, None, None] * ahead[None]
    diag = jnp.where(((s_i // CHUNK) <= (t_i // CHUNK))[None], corr, -jnp.inf)
    return kbias, diag


def _attn_sample_kernel(q_ref, kn_ref, vn_ref, ck_ref, cv_ref, lamp_ref, g_ref, o_ref,
                        *, t, past, lam_init):
    t_pos = past + lax.broadcasted_iota(jnp.int32, (2 * t, 1), 0) % t

    def neg_dist(s_pos):
        vis = (s_pos // CHUNK) <= (t_pos // CHUNK)
        return jnp.where(vis, -jnp.abs(t_pos - s_pos).astype(F32), -jnp.inf)

    nd_c = neg_dist(lax.broadcasted_iota(jnp.int32, (1, past), 1))
    nd_n = neg_dist(past + lax.broadcasted_iota(jnp.int32, (1, t), 1))
    lam = _lambda(lamp_ref, lam_init)
    for hd in range(N_DIFF_HEADS):
        slope = 2.0 ** (-8.0 * (hd + 1) / N_DIFF_HEADS)
        lanes = slice(hd * DIFF_V_DIM, (hd + 1) * DIFF_V_DIM)
        qs = jnp.concatenate(_split_q(q_ref[:, lanes]), axis=0)
        head_rows = pl.ds(hd, past, stride=N_DIFF_HEADS)
        kc = ck_ref[head_rows, :].astype(BF16)
        vc = cv_ref[head_rows, :].astype(BF16)
        sc = _qk(qs, kc) + slope * nd_c
        sn = _qk(qs, kn_ref[:, lanes]) + slope * nd_n
        m = jnp.maximum(jnp.max(sc, axis=-1, keepdims=True), jnp.max(sn, axis=-1, keepdims=True))
        pc = jnp.exp(sc - m)
        pn = jnp.exp(sn - m)
        l = jnp.sum(pc, axis=-1, keepdims=True) + jnp.sum(pn, axis=-1, keepdims=True)
        o = (jnp.dot(pc.astype(BF16), vc, preferred_element_type=F32)
             + jnp.dot(pn.astype(BF16), vn_ref[:, lanes], preferred_element_type=F32)) / l
        o_ref[:, lanes] = _diff_post(o[:t], o[t:], lam, g_ref[...], lam_init).astype(o_ref.dtype)


def _attn_sample(q, kb, vb, cache_k, cache_v, layer, lamp, g, lam_init, nb, t):
    past = cache_k.shape[2] // N_DIFF_HEADS
    tok = pl.BlockSpec((t, D_DIFF), lambda b: (b, 0))
    cache = pl.BlockSpec((None, None, past * N_DIFF_HEADS, DIFF_V_DIM),
                         lambda b: (layer, b, 0, 0))
    return pl.pallas_call(
        functools.partial(_attn_sample_kernel, t=t, past=past, lam_init=lam_init),
        grid=(nb,),
        in_specs=[tok, tok, tok, cache, cache,
                  pl.BlockSpec((4, DIFF_QK_DIM), lambda b: (0, 0)),
                  pl.BlockSpec((1, DIFF_V_DIM), lambda b: (0, 0))],
        out_specs=tok,
        out_shape=jax.ShapeDtypeStruct((nb * t, D_DIFF), BF16),
        compiler_params=_params(("parallel",)),
        name="attn_sample",
    )(q, kb, vb, cache_k, cache_v, lamp, g)


def _split_bf16(x):
    hi = x.astype(BF16)
    lo = (x - hi.astype(F32)).astype(BF16)
    return hi, lo


def _top2_sum(a, b, c, d):
    return jnp.maximum(jnp.maximum(jnp.maximum(a + b, a + c), jnp.maximum(a + d, b + c)),
                       jnp.maximum(b + d, c + d))


def _gates_t(aff, sel):
    rows = [sel[e:e + 1, :] for e in range(N_EXPERTS)]
    arow = [aff[e:e + 1, :] for e in range(N_EXPERTS)]
    gs = [_top2_sum(*rows[4 * g:4 * g + 4]) for g in range(N_GROUPS)]
    best = jnp.maximum(jnp.maximum(gs[0], gs[1]), jnp.maximum(gs[2], gs[3]))
    is_g = []
    taken = None
    for g in range(N_GROUPS):
        hit = gs[g] == best
        if taken is None:
            is_g.append(hit)
            taken = hit
        else:
            is_g.append(hit & ~taken)
            taken = taken | hit
    vals, affs = [], []
    for i in range(EXPERTS_PER_GROUP):
        v = rows[i]
        a = arow[i]
        for g in range(1, N_GROUPS):
            v = jnp.where(is_g[g], rows[4 * g + i], v)
            a = jnp.where(is_g[g], arow[4 * g + i], a)
        vals.append(v)
        affs.append(a)
    picked = []
    for i in range(EXPERTS_PER_GROUP):
        rank = jnp.zeros_like(vals[i])
        for j in range(EXPERTS_PER_GROUP):
            if j == i:
                continue
            ahead = (vals[j] >= vals[i]) if j < i else (vals[j] > vals[i])
            rank = rank + jnp.where(ahead, 1.0, 0.0)
        picked.append(rank < 2.0)
    wsum = jnp.zeros_like(affs[0])
    for i in range(EXPERTS_PER_GROUP):
        wsum = wsum + jnp.where(picked[i], affs[i], 0.0)
    row_id = lax.broadcasted_iota(jnp.int32, aff.shape, 0)
    gates = jnp.zeros(aff.shape, F32)
    for g in range(N_GROUPS):
        for i in range(EXPERTS_PER_GROUP):
            w = jnp.where(is_g[g] & picked[i], affs[i] / wsum, 0.0)
            gates = jnp.where(row_id == 4 * g + i, w, gates)
    return gates


def _mix_kernel(h_ref, oa_ref, ob_ref, wo_ref, bo_ref, g_ref, b_ref, wr_hi_ref, wr_lo_ref,
                rb_ref, h1_ref, h1b_ref, gates_ref, *, tm, sub):
    for r0 in range(0, tm, sub):
        rows = slice(r0, r0 + sub)
        mix = (jnp.dot(oa_ref[rows, :], wo_ref[:D_GMLP, :], preferred_element_type=F32)
               + jnp.dot(ob_ref[rows, :], wo_ref[D_GMLP:, :], preferred_element_type=F32)
               + bo_ref[...])
        h1 = _ln(ALPHA * h_ref[rows, :] + mix, g_ref[...], b_ref[...])
        h1_ref[rows, :] = h1
        h1b_ref[rows, :] = h1.astype(BF16)
        hi, lo = _split_bf16(h1)
        logits = (jnp.dot(hi, wr_hi_ref[...], preferred_element_type=F32)
                  + jnp.dot(lo, wr_hi_ref[...], preferred_element_type=F32)
                  + jnp.dot(hi, wr_lo_ref[...], preferred_element_type=F32))
        lt = logits.T[:N_EXPERTS, :]
        aff = 1.0 / (1.0 + jnp.exp(-lt))
        sel = aff + rb_ref[...]
        gates = _gates_t(aff, sel)
        gpad = jnp.concatenate([gates, jnp.zeros((LANES - N_EXPERTS, sub), F32)], axis=0)
        gates_ref[rows, :] = gpad.T


def _mix(h, o_a, o_b, w_out_b, layer, b_out, ln_g, ln_b, wr_hi, wr_lo, rb, tm):
    t = h.shape[0]
    row = lambda i: (i, 0)
    const2 = lambda i: (0, 0)
    return pl.pallas_call(
        functools.partial(_mix_kernel, tm=tm, sub=MIX_SUB_ROWS),
        grid=(t // tm,),
        in_specs=[pl.BlockSpec((tm, D_MODEL), row),
                  pl.BlockSpec((tm, D_GMLP), row),
                  pl.BlockSpec((tm, D_DIFF), row),
                  pl.BlockSpec((None, D_MODEL, D_MODEL), lambda i: (layer, 0, 0)),
                  pl.BlockSpec((1, D_MODEL), const2),
                  pl.BlockSpec((1, D_MODEL), const2),
                  pl.BlockSpec((1, D_MODEL), const2),
                  pl.BlockSpec((D_MODEL, LANES), const2),
                  pl.BlockSpec((D_MODEL, LANES), const2),
                  pl.BlockSpec((N_EXPERTS, 1), const2)],
        out_specs=[pl.BlockSpec((tm, D_MODEL), row),
                   pl.BlockSpec((tm, D_MODEL), row),
                   pl.BlockSpec((tm, LANES), row)],
        out_shape=[jax.ShapeDtypeStruct((t, D_MODEL), F32),
                   jax.ShapeDtypeStruct((t, D_MODEL), BF16),
                   jax.ShapeDtypeStruct((t, LANES), F32)],
        compiler_params=_params(("parallel",)),
        name="mix_router",
    )(h, o_a, o_b, w_out_b, b_out.reshape(1, -1), ln_g.reshape(1, -1), ln_b.reshape(1, -1),
      wr_hi, wr_lo, rb)


MIX_SUB_ROWS = 256
EXPERTS_PER_STEP = 2


def _gelu_gated(x, half_gate):
    c0 = math.sqrt(2.0 / math.pi)
    inner = x * (c0 + (c0 * 0.044715) * (x * x))
    return (x * (1.0 + jnp.tanh(inner))) * half_gate


def _moe_kernel(x_ref, gates_ref, h1_ref, wu_a_ref, wu_b_ref, bu_ref, wd_ref, bd_ref, g_ref,
                b_ref, o_ref, acc_ref):
    step = pl.program_id(1)

    @pl.when(step == 0)
    def _():
        acc_ref[...] = jnp.zeros(acc_ref.shape, F32)

    gates = gates_ref[...]
    lane = lax.broadcasted_iota(jnp.int32, gates.shape, 1)
    x = x_ref[...]
    down = None
    for n, wu_ref in enumerate((wu_a_ref, wu_b_ref)):
        e = EXPERTS_PER_STEP * step + n
        half_gate = 0.5 * jnp.sum(jnp.where(lane == e, gates, 0.0), axis=-1, keepdims=True)
        cols = slice(n * D_EXPERT, (n + 1) * D_EXPERT)
        up = jnp.dot(x, wu_ref[...], preferred_element_type=F32) + bu_ref[:, cols]
        hid = _gelu_gated(up, half_gate).astype(BF16)
        part = jnp.dot(hid, wd_ref[cols, :], preferred_element_type=F32)
        down = part if down is None else down + part
    acc_ref[...] += down

    @pl.when(step == N_EXPERTS // EXPERTS_PER_STEP - 1)
    def _():
        f = acc_ref[...] + jnp.dot(gates.astype(BF16), bd_ref[...], preferred_element_type=F32)
        o_ref[...] = _ln(ALPHA * h1_ref[...] + f, g_ref[...], b_ref[...])


def _moe(h1, h1b, gates, w_up_b, b_up, w_down_b, b_down_pad, layer, ln_g, ln_b, tm):
    t = h1.shape[0]
    n_steps = N_EXPERTS // EXPERTS_PER_STEP
    row = lambda i, p: (i, 0)
    const2 = lambda i, p: (0, 0)
    up_spec = lambda n: pl.BlockSpec((None, None, D_MODEL, D_EXPERT),
                                     lambda i, p: (layer, EXPERTS_PER_STEP * p + n, 0, 0))
    return pl.pallas_call(
        _moe_kernel,
        grid=(t // tm, n_steps),
        in_specs=[pl.BlockSpec((tm, D_MODEL), row),
                  pl.BlockSpec((tm, LANES), row),
                  pl.BlockSpec((tm, D_MODEL), row),
                  up_spec(0), up_spec(1),
                  pl.BlockSpec((None, None, 1, EXPERTS_PER_STEP * D_EXPERT),
                               lambda i, p: (layer, p, 0, 0)),
                  pl.BlockSpec((None, None, EXPERTS_PER_STEP * D_EXPERT, D_MODEL),
                               lambda i, p: (layer, p, 0, 0)),
                  pl.BlockSpec((None, LANES, D_MODEL), lambda i, p: (layer, 0, 0)),
                  pl.BlockSpec((1, D_MODEL), const2),
                  pl.BlockSpec((1, D_MODEL), const2)],
        out_specs=pl.BlockSpec((tm, D_MODEL), row),
        out_shape=jax.ShapeDtypeStruct((t, D_MODEL), F32),
        scratch_shapes=[pltpu.VMEM((tm, D_MODEL), F32)],
        compiler_params=_params(("parallel", "arbitrary")),
        name="moe_ln",
    )(h1b, gates, h1, w_up_b, w_up_b, b_up, w_down_b, b_down_pad,
      ln_g.reshape(1, -1), ln_b.reshape(1, -1))


def kernel(x_prompt, x_sample, cache_k, cache_v, ln_in_g, ln_in_b, w_in, w_out, b_out,
           gmlp_ln_g, gmlp_ln_b, gmlp_w_s, gmlp_b_s, lambda_q1, lambda_k1, lambda_q2,
           lambda_k2, subln_g, ln_mix_g, ln_mix_b, w_router, router_bias, w_up, b_up,
           w_down, b_down, ln_ffn_g, ln_ffn_b):
    nbp, seq, d = x_prompt.shape
    nbs, tdec, _ = x_sample.shape
    assert nbp == 1 and d == D_MODEL
    ts = nbs * tdec
    past = cache_k.shape[2]
    ck_rows = cache_k.reshape(DEPTH, nbs, past * N_DIFF_HEADS, DIFF_V_DIM)
    cv_rows = cache_v.reshape(DEPTH, nbs, past * N_DIFF_HEADS, DIFF_V_DIM)

    hp = _input_ln(x_prompt.reshape(seq, d), ln_in_g, ln_in_b, 1024)
    hs = _input_ln(x_sample.reshape(ts, d), ln_in_g, ln_in_b, ts)

    wr_pad = jnp.pad(w_router, ((0, 0), (0, LANES - N_EXPERTS)))
    wr_hi, wr_lo = (w.astype(BF16) for w in _bf16_terms(wr_pad, 2))
    rb = router_bias.astype(F32).reshape(N_EXPERTS, 1)
    kbias, diag = _alibi_tables(ATTN_TK, ATTN_TQ)

    w_in_b = w_in.astype(BF16)
    w_out_b = w_out.astype(BF16)
    w_up_b = w_up.astype(BF16)
    pairs = N_EXPERTS // EXPERTS_PER_STEP
    w_down_b = w_down.astype(BF16).reshape(DEPTH, pairs, EXPERTS_PER_STEP * D_EXPERT, D_MODEL)
    b_up4 = b_up.reshape(DEPTH, pairs, 1, EXPERTS_PER_STEP * D_EXPERT)
    b_down_pad = jnp.pad(b_down, ((0, 0), (0, LANES - N_EXPERTS), (0, 0))).astype(BF16)
    gln_g = gmlp_ln_g.reshape(DEPTH, 1, D_GMLP)
    gln_b = gmlp_ln_b.reshape(DEPTH, 1, D_GMLP)
    b_s_t = gmlp_b_s.transpose(0, 2, 1)

    kp, vp, ksm, vsm, gvs = [], [], [], [], []
    for l in range(DEPTH):
        lam_init = 0.8 - 0.6 * math.exp(-0.3 * l)
        lamp = jnp.stack([lambda_q1[l], lambda_k1[l], lambda_q2[l], lambda_k2[l]]).astype(F32)

        def tail(h, o_a, o_b, tm_mix, tm_moe):
            h1, h1b, gates = _mix(h, o_a, o_b, w_out_b, l, b_out[l], ln_mix_g[l], ln_mix_b[l],
                                  wr_hi, wr_lo, rb, tm_mix)
            return _moe(h1, h1b, gates, w_up_b, b_up4, w_down_b, b_down_pad, l,
                        ln_ffn_g[l], ln_ffn_b[l], tm_moe)

        o_a, qt1, qt2, ka1, ka2, vt, kf, vf = _proj_prompt(
            hp, w_in_b, gln_g, gln_b, gmlp_w_s, b_s_t, kbias, l, ATTN_TK)
        o_b = _attn_prompt(qt1, qt2, ka1, ka2, vt, diag, lamp,
                           subln_g[l].reshape(DIFF_V_DIM, 1), lam_init, ATTN_TQ, ATTN_TK)
        hp = tail(hp, o_a, o_b, 1024, 1024)
        kp.append(kf)
        vp.append(vf)

        o_a, q, kf, vf, kb, vb, vn = _proj_sample(hs, w_in_b, gln_g, gln_b, gmlp_w_s, b_s_t,
                                                  l, ts, tdec)
        o_b = _attn_sample(q, kb, vb, ck_rows, cv_rows, l, lamp,
                           subln_g[l].reshape(1, DIFF_V_DIM), lam_init, nbs, tdec)
        hs = tail(hs, o_a, o_b, ts, ts)
        ksm.append(kf)
        vsm.append(vf)
        gvs.append(vn)

    hd = (N_DIFF_HEADS, DIFF_V_DIM)
    return (hp.reshape(1, seq, d),
            hs.reshape(nbs, tdec, d),
            jnp.concatenate(kp, axis=0),
            jnp.concatenate(vp, axis=0),
            jnp.stack(ksm).reshape(DEPTH, nbs, tdec, *hd),
            jnp.stack(vsm).reshape(DEPTH, nbs, tdec, *hd),
            jnp.stack(gvs).reshape(DEPTH, nbs, tdec, D_GMLP))
```

```python
import functools
import math

import jax
import jax.numpy as jnp
from jax import lax
from jax.experimental import pallas as pl
from jax.experimental.pallas import tpu as pltpu

F32 = jnp.float32
BF16 = jnp.bfloat16

D_MODEL = 1024
DEPTH = 4
CHUNK = 64
D_GMLP = 512
GMLP_BLOCK = 128
GMLP_HEAD = 128
N_GMLP_HEADS = 4
D_DIFF = 512
N_DIFF_HEADS = 4
DIFF_V_DIM = 128
DIFF_QK_DIM = 64
OFF_Q = 2 * D_GMLP
OFF_K = OFF_Q + 512
OFF_V = OFF_K + 512
D_IN = OFF_V + D_DIFF
N_EXPERTS = 16
N_GROUPS = 4
EXPERTS_PER_GROUP = 4
D_EXPERT = 512
ALPHA = (2.0 * DEPTH) ** 0.25
LN_EPS = 1e-5
RMS_EPS = 1e-5
QK_SCALE = DIFF_QK_DIM ** -0.5
LOG2E = math.log2(math.e)
ATTN_TK = 512
ATTN_TQ = 1024
N_BIAS = 3
LANES = 128
VMEM_LIMIT = 48 * 1024 * 1024


def _ln(x, g, b):
    mu = jnp.mean(x, -1, keepdims=True)
    xc = x - mu
    var = jnp.mean(xc * xc, -1, keepdims=True)
    return xc * lax.rsqrt(var + LN_EPS) * g + b


def _params(sem):
    return pltpu.CompilerParams(dimension_semantics=sem, vmem_limit_bytes=VMEM_LIMIT)


def _ln_kernel(x_ref, g_ref, b_ref, o_ref):
    o_ref[...] = _ln(x_ref[...], g_ref[...], b_ref[...])


def _input_ln(x, g, b, tm):
    t, d = x.shape
    return pl.pallas_call(
        _ln_kernel,
        grid=(t // tm,),
        in_specs=[pl.BlockSpec((tm, d), lambda i: (i, 0)),
                  pl.BlockSpec((1, d), lambda i: (0, 0)),
                  pl.BlockSpec((1, d), lambda i: (0, 0))],
        out_specs=pl.BlockSpec((tm, d), lambda i: (i, 0)),
        out_shape=jax.ShapeDtypeStruct((t, d), F32),
        compiler_params=_params(("parallel",)),
        name="input_ln",
    )(x, g.reshape(1, d), b.reshape(1, d))


def _sgu(u, v_nb, ws_ref, bs_ref, oa_ref, tm, blk):
    ri = lax.broadcasted_iota(jnp.int32, (blk, blk), 0) // CHUNK
    ci = lax.broadcasted_iota(jnp.int32, (blk, blk), 1) // CHUNK
    vis = ci <= ri
    for g in range(N_GMLP_HEADS):
        w_g = jnp.where(vis, ws_ref[g, :blk, :blk], 0.0).astype(BF16)
        b_g = bs_ref[:blk, g:g + 1]
        c0 = g * GMLP_HEAD
        for r in range(tm // blk):
            r0 = r * blk
            s = jnp.dot(w_g, v_nb[r0:r0 + blk, c0:c0 + GMLP_HEAD],
                        preferred_element_type=F32) + b_g
            oa_ref[r0:r0 + blk, c0:c0 + GMLP_HEAD] = (
                u[r0:r0 + blk, c0:c0 + GMLP_HEAD] * s).astype(BF16)


def _proj_sample_kernel(h_ref, w_ref, lng_ref, lnb_ref, ws_ref, bs_ref,
                        oa_ref, q_ref, kf_ref, vf_ref, kb_ref, vb_ref, vn_ref, *, tm, blk):
    hb = h_ref[...].astype(BF16)

    def proj(off, width):
        return jnp.dot(hb, w_ref[:, off:off + width], preferred_element_type=F32)

    u = proj(0, D_GMLP)
    v_a = proj(D_GMLP, D_GMLP)
    q = proj(OFF_Q, 512)
    k = proj(OFF_K, 512)
    v = proj(OFF_V, D_DIFF)
    q_ref[...] = (q * QK_SCALE).astype(BF16)
    kf_ref[...] = k
    vf_ref[...] = v
    kb_ref[...] = k.astype(BF16)
    vb_ref[...] = v.astype(BF16)
    v_n = _ln(v_a, lng_ref[...], lnb_ref[...])
    vn_ref[...] = v_n
    _sgu(u, v_n.astype(BF16), ws_ref, bs_ref, oa_ref, tm, blk)


def _proj_prompt_kernel(h_ref, w_ref, lng_ref, lnb_ref, ws_ref, bs_ref, kbias_ref,
                        k_all_ref, v_all_ref,
                        oa_ref, qt1_ref, qt2_ref, ka1_ref, ka2_ref, vt_ref, kf_ref, vf_ref,
                        *, tm, blk):
    hb = h_ref[...].astype(BF16)

    def proj(off, width):
        return jnp.dot(hb, w_ref[:, off:off + width], preferred_element_type=F32)

    u = proj(0, D_GMLP)
    v_a = proj(D_GMLP, D_GMLP)
    q = proj(OFF_Q, 512)
    k = proj(OFF_K, 512)
    v = proj(OFF_V, D_DIFF)
    for hd in range(N_DIFF_HEADS):
        lanes = slice(hd * DIFF_V_DIM, (hd + 1) * DIFF_V_DIM)
        kf_ref[:, hd, :] = k[:, lanes]
        vf_ref[:, hd, :] = v[:, lanes]
    vt_ref[...] = v.T.astype(BF16)

    lane = lax.broadcasted_iota(jnp.int32, k.shape, 1) % DIFF_V_DIM
    kbias = kbias_ref[...]
    ka1_ref[...] = jnp.where(lane < DIFF_QK_DIM, k, kbias).astype(BF16)
    ka2_ref[...] = jnp.where(lane >= DIFF_QK_DIM, k, kbias).astype(BF16)

    qt = (q * (QK_SCALE * LOG2E)).T
    row = lax.broadcasted_iota(jnp.int32, qt.shape, 0) % DIFF_V_DIM
    ones1 = jnp.where((row >= DIFF_QK_DIM) & (row < DIFF_QK_DIM + N_BIAS), 1.0, 0.0)
    ones2 = jnp.where(row < N_BIAS, 1.0, 0.0)
    qt1_ref[...] = jnp.where(row < DIFF_QK_DIM, qt, ones1).astype(BF16)
    qt2_ref[...] = jnp.where(row >= DIFF_QK_DIM, qt, ones2).astype(BF16)

    v_n = _ln(v_a, lng_ref[...], lnb_ref[...])
    _sgu(u, v_n.astype(BF16), ws_ref, bs_ref, oa_ref, tm, blk)


def _proj_in_specs(tm, layer):
    row = lambda i: (i, 0)
    per_layer = lambda i: (layer, 0, 0)
    return [pl.BlockSpec((tm, D_MODEL), row),
            pl.BlockSpec((None, D_MODEL, D_IN), per_layer),
            pl.BlockSpec((None, 1, D_GMLP), per_layer),
            pl.BlockSpec((None, 1, D_GMLP), per_layer),
            pl.BlockSpec((None, N_GMLP_HEADS, GMLP_BLOCK, GMLP_BLOCK), lambda i: (layer, 0, 0, 0)),
            pl.BlockSpec((None, GMLP_BLOCK, N_GMLP_HEADS), per_layer)]


def _proj_sample(h, w_in_b, ln_g, ln_b, w_s, b_s_t, layer, tm, blk):
    t = h.shape[0]
    row = lambda i: (i, 0)
    return pl.pallas_call(
        functools.partial(_proj_sample_kernel, tm=tm, blk=blk),
        grid=(t // tm,),
        in_specs=_proj_in_specs(tm, layer),
        out_specs=[pl.BlockSpec((tm, 512), row)] * 7,
        out_shape=[jax.ShapeDtypeStruct((t, 512), BF16),
                   jax.ShapeDtypeStruct((t, 512), BF16),
                   jax.ShapeDtypeStruct((t, 512), F32),
                   jax.ShapeDtypeStruct((t, 512), F32),
                   jax.ShapeDtypeStruct((t, 512), BF16),
                   jax.ShapeDtypeStruct((t, 512), BF16),
                   jax.ShapeDtypeStruct((t, 512), F32)],
        compiler_params=_params(("parallel",)),
        name="proj_gmlp_sample",
    )(h, w_in_b, ln_g, ln_b, w_s, b_s_t)


def _proj_prompt(h, w_in_b, ln_g, ln_b, w_s, b_s_t, kbias, k_all, v_all, layer, tm):
    t = h.shape[0]
    row = lambda i: (i, 0)
    col = lambda i: (0, i)
    rows_bf = jax.ShapeDtypeStruct((t, 512), BF16)
    cols_bf = jax.ShapeDtypeStruct((512, t), BF16)
    heads_f = jax.ShapeDtypeStruct(k_all.shape, F32)
    heads_spec = pl.BlockSpec((None, None, tm, N_DIFF_HEADS, DIFF_V_DIM),
                              lambda i: (layer, 0, i, 0, 0))
    in_specs = _proj_in_specs(tm, layer) + [pl.BlockSpec((tm, 512), lambda i: (0, 0)),
                                            pl.BlockSpec(memory_space=pl.ANY),
                                            pl.BlockSpec(memory_space=pl.ANY)]
    return pl.pallas_call(
        functools.partial(_proj_prompt_kernel, tm=tm, blk=GMLP_BLOCK),
        grid=(t // tm,),
        in_specs=in_specs,
        input_output_aliases={len(in_specs) - 2: 6, len(in_specs) - 1: 7},
        out_specs=[pl.BlockSpec((tm, 512), row),
                   pl.BlockSpec((512, tm), col),
                   pl.BlockSpec((512, tm), col),
                   pl.BlockSpec((tm, 512), row),
                   pl.BlockSpec((tm, 512), row),
                   pl.BlockSpec((512, tm), col),
                   heads_spec,
                   heads_spec],
        out_shape=[rows_bf, cols_bf, cols_bf, rows_bf, rows_bf, cols_bf, heads_f, heads_f],
        compiler_params=_params(("parallel",)),
        name="proj_gmlp_prompt",
    )(h, w_in_b, ln_g, ln_b, w_s, b_s_t, kbias, k_all, v_all)


def _lambda(lamp_ref, lam_init):
    lp = lamp_ref[...]
    d1 = jnp.sum(lp[0:1] * lp[1:2], axis=-1, keepdims=True)
    d2 = jnp.sum(lp[2:3] * lp[3:4], axis=-1, keepdims=True)
    return jnp.exp(d1) - jnp.exp(d2) + lam_init


def _head_slope(h):
    return jnp.exp2(-(8.0 / N_DIFF_HEADS) * (h + 1).astype(F32))


def _split_q(q):
    lane = lax.broadcasted_iota(jnp.int32, q.shape, 1)
    zero = jnp.zeros_like(q)
    return jnp.where(lane < DIFF_QK_DIM, q, zero), jnp.where(lane >= DIFF_QK_DIM, q, zero)


def _qk(qz, kc):
    return lax.dot_general(qz, kc, (((1,), (1,)), ((), ())), preferred_element_type=F32)


def _diff_post(o1, o2, lam, g, lam_init):
    o = o1 - lam * o2
    o = o * lax.rsqrt(jnp.mean(o * o, -1, keepdims=True) + RMS_EPS) * g
    return o * (1.0 - lam_init)


class _MapState:
    def __init__(self, qt, ka, m, l, acc, s_bufs, p_bufs, shift, alpha_s, alpha_p):
        self.qt, self.ka, self.m, self.l, self.acc = qt, ka, m, l, acc
        self.s_bufs, self.p_bufs = s_bufs, p_bufs
        self.shift, self.alpha_s, self.alpha_p = shift, alpha_s, alpha_p


def _attn_prompt_kernel(qt1_ref, qt2_ref, ka1_ref, ka2_ref, vt_ref, diag_ref, lamp_ref, g_ref,
                        o_ref, *scratch, tq, tk, lam_init):
    h = pl.program_id(0)
    i = pl.program_id(1)
    chunk_bias = _head_slope(h) * (LOG2E * tk)
    n_diag = tq // tk
    first_diag = i * n_diag
    per_map = len(scratch) // 2
    maps = []
    for n, (qt, ka) in enumerate(((qt1_ref, ka1_ref), (qt2_ref, ka2_ref))):
        m, l, acc, s0, s1, p0, p1, shift, alpha_s, alpha_p = scratch[n * per_map:(n + 1) * per_map]
        maps.append(_MapState(qt, ka, m, l, acc, (s0, s1), (p0, p1), shift, alpha_s, alpha_p))
        m[...] = jnp.full(m.shape, -jnp.inf, F32)
        l[...] = jnp.zeros(l.shape, F32)
        acc[...] = jnp.zeros(acc.shape, F32)

    def chunk_of(idx):
        j = jnp.where(idx < n_diag, first_diag + idx, idx - n_diag)
        return j, chunk_bias * j.astype(F32)

    def scores(idx, slot, diagonal=None):
        j, c_j = chunk_of(idx)
        start = pl.multiple_of(j * tk, tk)
        for st in maps:
            s = jnp.dot(st.ka[pl.ds(start, tk), :], st.qt[...],
                        preferred_element_type=F32)
            if diagonal is not None:
                s = s + diag_ref[diagonal * tk:(diagonal + 1) * tk, :]
            st.s_bufs[slot][...] = s
            m_prev = st.m[...]
            m_new = jnp.maximum(m_prev, jnp.max(s, axis=0, keepdims=True) + c_j)
            st.alpha_s[slot:slot + 1, :] = jnp.exp2(m_prev - m_new)
            st.shift[slot:slot + 1, :] = m_new - c_j
            st.m[...] = m_new

    def exponentials(slot):
        for st in maps:
            alpha = st.alpha_s[slot:slot + 1, :]
            p = jnp.exp2(st.s_bufs[slot][...] - st.shift[slot:slot + 1, :])
            st.l[...] = alpha * st.l[...] + jnp.sum(p, axis=0, keepdims=True)
            st.p_bufs[slot][...] = p.astype(BF16)
            st.alpha_p[slot:slot + 1, :] = alpha

    def values(idx, slot):
        j, _ = chunk_of(idx)
        start = pl.multiple_of(j * tk, tk)
        vtc = vt_ref[:, pl.ds(start, tk)]
        for st in maps:
            st.acc[...] = (st.alpha_p[slot:slot + 1, :] * st.acc[...]
                           + jnp.dot(vtc, st.p_bufs[slot][...], preferred_element_type=F32))

    scores(0, 0, diagonal=0)
    scores(1, 1, diagonal=1)
    exponentials(0)

    def body(pair, carry):
        k = 2 * pair
        scores(k + 2, 0)
        exponentials(1)
        values(k, 0)
        scores(k + 3, 1)
        exponentials(0)
        values(k + 1, 1)
        return carry

    lax.fori_loop(0, i, body, 0)
    last = 2 * i
    exponentials(1)
    values(last, 0)
    values(last + 1, 1)

    lam = _lambda(lamp_ref, lam_init)
    st1, st2 = maps
    o = st1.acc[...] / st1.l[...] - lam * (st2.acc[...] / st2.l[...])
    o = o * lax.rsqrt(jnp.mean(o * o, axis=0, keepdims=True) + RMS_EPS) * g_ref[...]
    o_ref[...] = (o * (1.0 - lam_init)).T.astype(o_ref.dtype)


def _attn_prompt(qt1, qt2, ka1, ka2, vt, diag, lamp, g_col, lam_init, tq, tk):
    s = ka1.shape[0]
    assert tq == 2 * tk
    once = pl.Buffered(1)
    q_spec = pl.BlockSpec((DIFF_V_DIM, tq), lambda h, i: (h, i))
    k_spec = pl.BlockSpec((s, DIFF_V_DIM), lambda h, i: (0, h), pipeline_mode=once)
    row = pltpu.VMEM((1, tq), F32)
    two_rows = pltpu.VMEM((2, tq), F32)
    score = pltpu.VMEM((tk, tq), F32)
    prob = pltpu.VMEM((tk, tq), BF16)
    map_scratch = [row, row, pltpu.VMEM((DIFF_V_DIM, tq), F32), score, score, prob, prob,
                   two_rows, two_rows, two_rows]
    return pl.pallas_call(
        functools.partial(_attn_prompt_kernel, tq=tq, tk=tk, lam_init=lam_init),
        grid=(N_DIFF_HEADS, s // tq),
        in_specs=[q_spec, q_spec, k_spec, k_spec,
                  pl.BlockSpec((DIFF_V_DIM, s), lambda h, i: (h, 0), pipeline_mode=once),
                  pl.BlockSpec((None, tq, tq), lambda h, i: (h, 0, 0), pipeline_mode=once),
                  pl.BlockSpec((4, DIFF_QK_DIM), lambda h, i: (0, 0)),
                  pl.BlockSpec((DIFF_V_DIM, 1), lambda h, i: (0, 0))],
        out_specs=pl.BlockSpec((tq, DIFF_V_DIM), lambda h, i: (i, h)),
        out_shape=jax.ShapeDtypeStruct((s, D_DIFF), BF16),
        scratch_shapes=map_scratch * 2,
        compiler_params=_params(("parallel", "arbitrary")),
        name="attn_prompt",
    )(qt1, qt2, ka1, ka2, vt, diag, lamp, g_col)


def _bf16_terms(x, n):
    terms = []
    rem = x
    for _ in range(n):
        part = lax.bitcast_convert_type(
            lax.bitcast_convert_type(rem, jnp.uint32) & jnp.uint32(0xFFFF0000), F32)
        terms.append(part)
        rem = rem - part
    return terms


def _alibi_tables(tk, tq):
    slopes = jnp.asarray([2.0 ** (-8.0 * (h + 1) / N_DIFF_HEADS) for h in range(N_DIFF_HEADS)], F32)
    pos = jnp.arange(tk, dtype=F32)
    f = (slopes[:, None] * LOG2E) * pos[None, :]
    terms = jnp.stack(_bf16_terms(f, N_BIAS), axis=-1)
    head = jnp.zeros((N_DIFF_HEADS, tk, DIFF_V_DIM), F32)
    head = head.at[:, :, DIFF_QK_DIM:DIFF_QK_DIM + N_BIAS].set(terms)
    head = head.at[:, :, :N_BIAS].set(terms)
    kbias = head.transpose(1, 0, 2).reshape(tk, N_DIFF_HEADS * DIFF_V_DIM)
    s_i = jnp.arange(tq)[:, None]
    t_i = jnp.arange(tq)[None, :]
    ahead = jnp.maximum(s_i - t_i, 0).astype(F32)
    corr = (-2.0 * LOG2E) * slopes[:, None, None] * ahead[None]
    diag = jnp.where(((s_i // CHUNK) <= (t_i // CHUNK))[None], corr, -jnp.inf)
    return kbias, diag


def _attn_sample_kernel(q_ref, kn_ref, vn_ref, ck_ref, cv_ref, lamp_ref, g_ref, o_ref,
                        *, t, past, lam_init):
    t_pos = past + lax.broadcasted_iota(jnp.int32, (2 * t, 1), 0) % t

    def neg_dist(s_pos):
        vis = (s_pos // CHUNK) <= (t_pos // CHUNK)
        return jnp.where(vis, -jnp.abs(t_pos - s_pos).astype(F32), -jnp.inf)

    nd_c = neg_dist(lax.broadcasted_iota(jnp.int32, (1, past), 1))
    nd_n = neg_dist(past + lax.broadcasted_iota(jnp.int32, (1, t), 1))
    lam = _lambda(lamp_ref, lam_init)
    for hd in range(N_DIFF_HEADS):
        slope = 2.0 ** (-8.0 * (hd + 1) / N_DIFF_HEADS)
        lanes = slice(hd * DIFF_V_DIM, (hd + 1) * DIFF_V_DIM)
        qs = jnp.concatenate(_split_q(q_ref[:, lanes]), axis=0)
        head_rows = pl.ds(hd, past, stride=N_DIFF_HEADS)
        kc = ck_ref[head_rows, :].astype(BF16)
        vc = cv_ref[head_rows, :].astype(BF16)
        sc = _qk(qs, kc) + slope * nd_c
        sn = _qk(qs, kn_ref[:, lanes]) + slope * nd_n
        m = jnp.maximum(jnp.max(sc, axis=-1, keepdims=True), jnp.max(sn, axis=-1, keepdims=True))
        pc = jnp.exp(sc - m)
        pn = jnp.exp(sn - m)
        l = jnp.sum(pc, axis=-1, keepdims=True) + jnp.sum(pn, axis=-1, keepdims=True)
        o = (jnp.dot(pc.astype(BF16), vc, preferred_element_type=F32)
             + jnp.dot(pn.astype(BF16), vn_ref[:, lanes], preferred_element_type=F32)) / l
        o_ref[:, lanes] = _diff_post(o[:t], o[t:], lam, g_ref[...], lam_init).astype(o_ref.dtype)


def _attn_sample(q, kb, vb, cache_k, cache_v, layer, lamp, g, lam_init, nb, t):
    past = cache_k.shape[2] // N_DIFF_HEADS
    tok = pl.BlockSpec((t, D_DIFF), lambda b: (b, 0))
    cache = pl.BlockSpec((None, None, past * N_DIFF_HEADS, DIFF_V_DIM),
                         lambda b: (layer, b, 0, 0))
    return pl.pallas_call(
        functools.partial(_attn_sample_kernel, t=t, past=past, lam_init=lam_init),
        grid=(nb,),
        in_specs=[tok, tok, tok, cache, cache,
                  pl.BlockSpec((4, DIFF_QK_DIM), lambda b: (0, 0)),
                  pl.BlockSpec((1, DIFF_V_DIM), lambda b: (0, 0))],
        out_specs=tok,
        out_shape=jax.ShapeDtypeStruct((nb * t, D_DIFF), BF16),
        compiler_params=_params(("parallel",)),
        name="attn_sample",
    )(q, kb, vb, cache_k, cache_v, lamp, g)


def _split_bf16(x):
    hi = x.astype(BF16)
    lo = (x - hi.astype(F32)).astype(BF16)
    return hi, lo


def _top2_sum(a, b, c, d):
    return jnp.maximum(jnp.maximum(jnp.maximum(a + b, a + c), jnp.maximum(a + d, b + c)),
                       jnp.maximum(b + d, c + d))


def _gates_t(aff, sel):
    rows = [sel[e:e + 1, :] for e in range(N_EXPERTS)]
    arow = [aff[e:e + 1, :] for e in range(N_EXPERTS)]
    gs = [_top2_sum(*rows[4 * g:4 * g + 4]) for g in range(N_GROUPS)]
    best = jnp.maximum(jnp.maximum(gs[0], gs[1]), jnp.maximum(gs[2], gs[3]))
    is_g = []
    taken = None
    for g in range(N_GROUPS):
        hit = gs[g] == best
        if taken is None:
            is_g.append(hit)
            taken = hit
        else:
            is_g.append(hit & ~taken)
            taken = taken | hit
    vals, affs = [], []
    for i in range(EXPERTS_PER_GROUP):
        v = rows[i]
        a = arow[i]
        for g in range(1, N_GROUPS):
            v = jnp.where(is_g[g], rows[4 * g + i], v)
            a = jnp.where(is_g[g], arow[4 * g + i], a)
        vals.append(v)
        affs.append(a)
    picked = []
    for i in range(EXPERTS_PER_GROUP):
        rank = jnp.zeros_like(vals[i])
        for j in range(EXPERTS_PER_GROUP):
            if j == i:
                continue
            ahead = (vals[j] >= vals[i]) if j < i else (vals[j] > vals[i])
            rank = rank + jnp.where(ahead, 1.0, 0.0)
        picked.append(rank < 2.0)
    wsum = jnp.zeros_like(affs[0])
    for i in range(EXPERTS_PER_GROUP):
        wsum = wsum + jnp.where(picked[i], affs[i], 0.0)
    row_id = lax.broadcasted_iota(jnp.int32, aff.shape, 0)
    gates = jnp.zeros(aff.shape, F32)
    for g in range(N_GROUPS):
        for i in range(EXPERTS_PER_GROUP):
            w = jnp.where(is_g[g] & picked[i], affs[i] / wsum, 0.0)
            gates = jnp.where(row_id == 4 * g + i, w, gates)
    return gates


def _mix_kernel(h_ref, oa_ref, ob_ref, wo_ref, bo_ref, g_ref, b_ref, wr_hi_ref, wr_lo_ref,
                rb_ref, h1_ref, h1b_ref, gates_ref, *, tm, sub):
    for r0 in range(0, tm, sub):
        rows = slice(r0, r0 + sub)
        mix = (jnp.dot(oa_ref[rows, :], wo_ref[:D_GMLP, :], preferred_element_type=F32)
               + jnp.dot(ob_ref[rows, :], wo_ref[D_GMLP:, :], preferred_element_type=F32)
               + bo_ref[...])
        h1 = _ln(ALPHA * h_ref[rows, :] + mix, g_ref[...], b_ref[...])
        h1_ref[rows, :] = h1
        h1b_ref[rows, :] = h1.astype(BF16)
        hi, lo = _split_bf16(h1)
        logits = (jnp.dot(hi, wr_hi_ref[...], preferred_element_type=F32)
                  + jnp.dot(lo, wr_hi_ref[...], preferred_element_type=F32)
                  + jnp.dot(hi, wr_lo_ref[...], preferred_element_type=F32))
        lt = logits.T[:N_EXPERTS, :]
        aff = 1.0 / (1.0 + jnp.exp(-lt))
        sel = aff + rb_ref[...]
        gates = _gates_t(aff, sel)
        gpad = jnp.concatenate([gates, jnp.zeros((LANES - N_EXPERTS, sub), F32)], axis=0)
        gates_ref[rows, :] = gpad.T


def _mix(h, o_a, o_b, w_out_b, layer, b_out, ln_g, ln_b, wr_hi, wr_lo, rb, tm):
    t = h.shape[0]
    row = lambda i: (i, 0)
    const2 = lambda i: (0, 0)
    return pl.pallas_call(
        functools.partial(_mix_kernel, tm=tm, sub=MIX_SUB_ROWS),
        grid=(t // tm,),
        in_specs=[pl.BlockSpec((tm, D_MODEL), row),
                  pl.BlockSpec((tm, D_GMLP), row),
                  pl.BlockSpec((tm, D_DIFF), row),
                  pl.BlockSpec((None, D_MODEL, D_MODEL), lambda i: (layer, 0, 0)),
                  pl.BlockSpec((1, D_MODEL), const2),
                  pl.BlockSpec((1, D_MODEL), const2),
                  pl.BlockSpec((1, D_MODEL), const2),
                  pl.BlockSpec((D_MODEL, LANES), const2),
                  pl.BlockSpec((D_MODEL, LANES), const2),
                  pl.BlockSpec((N_EXPERTS, 1), const2)],
        out_specs=[pl.BlockSpec((tm, D_MODEL), row),
                   pl.BlockSpec((tm, D_MODEL), row),
                   pl.BlockSpec((tm, LANES), row)],
        out_shape=[jax.ShapeDtypeStruct((t, D_MODEL), F32),
                   jax.ShapeDtypeStruct((t, D_MODEL), BF16),
                   jax.ShapeDtypeStruct((t, LANES), F32)],
        compiler_params=_params(("parallel",)),
        name="mix_router",
    )(h, o_a, o_b, w_out_b, b_out.reshape(1, -1), ln_g.reshape(1, -1), ln_b.reshape(1, -1),
      wr_hi, wr_lo, rb)


MIX_SUB_ROWS = 256
EXPERTS_PER_STEP = 2


def _gelu_gated(x, half_gate):
    c0 = math.sqrt(2.0 / math.pi)
    inner = x * (c0 + (c0 * 0.044715) * (x * x))
    return (x * (1.0 + jnp.tanh(inner))) * half_gate


def _moe_kernel(x_ref, gates_ref, h1_ref, *refs):
    wu_refs = refs[:EXPERTS_PER_STEP]
    bu_ref, wd_ref, bd_ref, g_ref, b_ref, o_ref, acc_ref = refs[EXPERTS_PER_STEP:]
    step = pl.program_id(1)

    @pl.when(step == 0)
    def _():
        acc_ref[...] = jnp.zeros(acc_ref.shape, F32)

    gates = gates_ref[...]
    lane = lax.broadcasted_iota(jnp.int32, gates.shape, 1)
    x = x_ref[...]
    down = None
    for n, wu_ref in enumerate(wu_refs):
        e = EXPERTS_PER_STEP * step + n
        half_gate = 0.5 * jnp.sum(jnp.where(lane == e, gates, 0.0), axis=-1, keepdims=True)
        cols = slice(n * D_EXPERT, (n + 1) * D_EXPERT)
        up = jnp.dot(x, wu_ref[...], preferred_element_type=F32) + bu_ref[:, cols]
        hid = _gelu_gated(up, half_gate).astype(BF16)
        part = jnp.dot(hid, wd_ref[cols, :], preferred_element_type=F32)
        down = part if down is None else down + part
    acc_ref[...] += down

    @pl.when(step == N_EXPERTS // EXPERTS_PER_STEP - 1)
    def _():
        f = acc_ref[...] + jnp.dot(gates.astype(BF16), bd_ref[...], preferred_element_type=F32)
        o_ref[...] = _ln(ALPHA * h1_ref[...] + f, g_ref[...], b_ref[...])


def _moe(h1, h1b, gates, w_up_b, b_up, w_down_b, b_down_pad, layer, ln_g, ln_b, tm):
    t = h1.shape[0]
    n_steps = N_EXPERTS // EXPERTS_PER_STEP
    row = lambda i, p: (i, 0)
    const2 = lambda i, p: (0, 0)
    up_spec = lambda n: pl.BlockSpec((None, None, D_MODEL, D_EXPERT),
                                     lambda i, p: (layer, EXPERTS_PER_STEP * p + n, 0, 0))
    return pl.pallas_call(
        _moe_kernel,
        grid=(t // tm, n_steps),
        in_specs=[pl.BlockSpec((tm, D_MODEL), row),
                  pl.BlockSpec((tm, LANES), row),
                  pl.BlockSpec((tm, D_MODEL), row),
                  *[up_spec(n) for n in range(EXPERTS_PER_STEP)],
                  pl.BlockSpec((None, None, 1, EXPERTS_PER_STEP * D_EXPERT),
                               lambda i, p: (layer, p, 0, 0)),
                  pl.BlockSpec((None, None, EXPERTS_PER_STEP * D_EXPERT, D_MODEL),
                               lambda i, p: (layer, p, 0, 0)),
                  pl.BlockSpec((None, LANES, D_MODEL), lambda i, p: (layer, 0, 0)),
                  pl.BlockSpec((1, D_MODEL), const2),
                  pl.BlockSpec((1, D_MODEL), const2)],
        out_specs=pl.BlockSpec((tm, D_MODEL), row),
        out_shape=jax.ShapeDtypeStruct((t, D_MODEL), F32),
        scratch_shapes=[pltpu.VMEM((tm, D_MODEL), F32)],
        compiler_params=_params(("parallel", "arbitrary")),
        name="moe_ln",
    )(h1b, gates, h1, *[w_up_b] * EXPERTS_PER_STEP, b_up, w_down_b, b_down_pad,
      ln_g.reshape(1, -1), ln_b.reshape(1, -1))


def kernel(x_prompt, x_sample, cache_k, cache_v, ln_in_g, ln_in_b, w_in, w_out, b_out,
           gmlp_ln_g, gmlp_ln_b, gmlp_w_s, gmlp_b_s, lambda_q1, lambda_k1, lambda_q2,
           lambda_k2, subln_g, ln_mix_g, ln_mix_b, w_router, router_bias, w_up, b_up,
           w_down, b_down, ln_ffn_g, ln_ffn_b):
    nbp, seq, d = x_prompt.shape
    nbs, tdec, _ = x_sample.shape
    assert nbp == 1 and d == D_MODEL
    ts = nbs * tdec
    past = cache_k.shape[2]
    ck_rows = cache_k.reshape(DEPTH, nbs, past * N_DIFF_HEADS, DIFF_V_DIM)
    cv_rows = cache_v.reshape(DEPTH, nbs, past * N_DIFF_HEADS, DIFF_V_DIM)

    hp = _input_ln(x_prompt.reshape(seq, d), ln_in_g, ln_in_b, 1024)
    hs = _input_ln(x_sample.reshape(ts, d), ln_in_g, ln_in_b, ts)

    wr_pad = jnp.pad(w_router, ((0, 0), (0, LANES - N_EXPERTS)))
    wr_hi, wr_lo = (w.astype(BF16) for w in _bf16_terms(wr_pad, 2))
    rb = router_bias.astype(F32).reshape(N_EXPERTS, 1)
    kbias, diag = _alibi_tables(ATTN_TK, ATTN_TQ)

    w_in_b = w_in.astype(BF16)
    w_out_b = w_out.astype(BF16)
    w_up_b = w_up.astype(BF16)
    steps = N_EXPERTS // EXPERTS_PER_STEP
    w_down_b = w_down.astype(BF16).reshape(DEPTH, steps, EXPERTS_PER_STEP * D_EXPERT, D_MODEL)
    b_up4 = b_up.reshape(DEPTH, steps, 1, EXPERTS_PER_STEP * D_EXPERT)
    b_down_pad = jnp.pad(b_down, ((0, 0), (0, LANES - N_EXPERTS), (0, 0))).astype(BF16)
    gln_g = gmlp_ln_g.reshape(DEPTH, 1, D_GMLP)
    gln_b = gmlp_ln_b.reshape(DEPTH, 1, D_GMLP)
    b_s_t = gmlp_b_s.transpose(0, 2, 1)

    k_all = jnp.zeros((DEPTH, 1, seq, N_DIFF_HEADS, DIFF_V_DIM), F32)
    v_all = jnp.zeros((DEPTH, 1, seq, N_DIFF_HEADS, DIFF_V_DIM), F32)
    ksm, vsm, gvs = [], [], []
    for l in range(DEPTH):
        lam_init = 0.8 - 0.6 * math.exp(-0.3 * l)
        lamp = jnp.stack([lambda_q1[l], lambda_k1[l], lambda_q2[l], lambda_k2[l]]).astype(F32)

        def tail(h, o_a, o_b, tm_mix, tm_moe):
            h1, h1b, gates = _mix(h, o_a, o_b, w_out_b, l, b_out[l], ln_mix_g[l], ln_mix_b[l],
                                  wr_hi, wr_lo, rb, tm_mix)
            return _moe(h1, h1b, gates, w_up_b, b_up4, w_down_b, b_down_pad, l,
                        ln_ffn_g[l], ln_ffn_b[l], tm_moe)

        o_a, qt1, qt2, ka1, ka2, vt, k_all, v_all = _proj_prompt(
            hp, w_in_b, gln_g, gln_b, gmlp_w_s, b_s_t, kbias, k_all, v_all, l, ATTN_TK)
        o_b = _attn_prompt(qt1, qt2, ka1, ka2, vt, diag, lamp,
                           subln_g[l].reshape(DIFF_V_DIM, 1), lam_init, ATTN_TQ, ATTN_TK)
        hp = tail(hp, o_a, o_b, 1024, 1024)

        o_a, q, kf, vf, kb, vb, vn = _proj_sample(hs, w_in_b, gln_g, gln_b, gmlp_w_s, b_s_t,
                                                  l, ts, tdec)
        o_b = _attn_sample(q, kb, vb, ck_rows, cv_rows, l, lamp,
                           subln_g[l].reshape(1, DIFF_V_DIM), lam_init, nbs, tdec)
        hs = tail(hs, o_a, o_b, ts, ts)
        ksm.append(kf)
        vsm.append(vf)
        gvs.append(vn)

    hd = (N_DIFF_HEADS, DIFF_V_DIM)
    return (hp.reshape(1, seq, d),
            hs.reshape(nbs, tdec, d),
            k_all,
            v_all,
            jnp.stack(ksm).reshape(DEPTH, nbs, tdec, *hd),
            jnp.stack(vsm).reshape(DEPTH, nbs, tdec, *hd),
            jnp.stack(gvs).reshape(DEPTH, nbs, tdec, D_GMLP))
```

```python
import functools
import math

import jax
import jax.numpy as jnp
from jax import lax
from jax.experimental import pallas as pl
from jax.experimental.pallas import tpu as pltpu

F32 = jnp.float32
BF16 = jnp.bfloat16

D_MODEL = 1024
DEPTH = 4
CHUNK = 64
D_GMLP = 512
GMLP_BLOCK = 128
GMLP_HEAD = 128
N_GMLP_HEADS = 4
D_DIFF = 512
N_DIFF_HEADS = 4
DIFF_V_DIM = 128
DIFF_QK_DIM = 64
OFF_Q = 2 * D_GMLP
OFF_K = OFF_Q + 512
OFF_V = OFF_K + 512
D_IN = OFF_V + D_DIFF
N_EXPERTS = 16
N_GROUPS = 4
EXPERTS_PER_GROUP = 4
D_EXPERT = 512
ALPHA = (2.0 * DEPTH) ** 0.25
LN_EPS = 1e-5
RMS_EPS = 1e-5
QK_SCALE = DIFF_QK_DIM ** -0.5
LOG2E = math.log2(math.e)
ATTN_TK = 512
ATTN_TQ = 1024
SUM_ROWS = 16
N_BIAS = 3
LANES = 128
VMEM_LIMIT = 48 * 1024 * 1024


def _ln(x, g, b):
    mu = jnp.mean(x, -1, keepdims=True)
    xc = x - mu
    var = jnp.mean(xc * xc, -1, keepdims=True)
    return xc * lax.rsqrt(var + LN_EPS) * g + b


def _params(sem):
    return pltpu.CompilerParams(dimension_semantics=sem, vmem_limit_bytes=VMEM_LIMIT)


def _ln_kernel(x_ref, g_ref, b_ref, o_ref):
    o_ref[...] = _ln(x_ref[...], g_ref[...], b_ref[...])


def _input_ln(x, g, b, tm):
    t, d = x.shape
    return pl.pallas_call(
        _ln_kernel,
        grid=(t // tm,),
        in_specs=[pl.BlockSpec((tm, d), lambda i: (i, 0)),
                  pl.BlockSpec((1, d), lambda i: (0, 0)),
                  pl.BlockSpec((1, d), lambda i: (0, 0))],
        out_specs=pl.BlockSpec((tm, d), lambda i: (i, 0)),
        out_shape=jax.ShapeDtypeStruct((t, d), F32),
        compiler_params=_params(("parallel",)),
        name="input_ln",
    )(x, g.reshape(1, d), b.reshape(1, d))


def _sgu(u, v_nb, ws_ref, bs_ref, oa_ref, tm, blk):
    ri = lax.broadcasted_iota(jnp.int32, (blk, blk), 0) // CHUNK
    ci = lax.broadcasted_iota(jnp.int32, (blk, blk), 1) // CHUNK
    vis = ci <= ri
    for g in range(N_GMLP_HEADS):
        w_g = jnp.where(vis, ws_ref[g, :blk, :blk], 0.0).astype(BF16)
        b_g = bs_ref[:blk, g:g + 1]
        c0 = g * GMLP_HEAD
        for r in range(tm // blk):
            r0 = r * blk
            s = jnp.dot(w_g, v_nb[r0:r0 + blk, c0:c0 + GMLP_HEAD],
                        preferred_element_type=F32) + b_g
            oa_ref[r0:r0 + blk, c0:c0 + GMLP_HEAD] = (
                u[r0:r0 + blk, c0:c0 + GMLP_HEAD] * s).astype(BF16)


def _proj_sample_kernel(h_ref, w_ref, lng_ref, lnb_ref, ws_ref, bs_ref,
                        oa_ref, q_ref, kf_ref, vf_ref, kb_ref, vb_ref, vn_ref, *, tm, blk):
    hb = h_ref[...].astype(BF16)

    def proj(off, width):
        return jnp.dot(hb, w_ref[:, off:off + width], preferred_element_type=F32)

    u = proj(0, D_GMLP)
    v_a = proj(D_GMLP, D_GMLP)
    q = proj(OFF_Q, 512)
    k = proj(OFF_K, 512)
    v = proj(OFF_V, D_DIFF)
    q_ref[...] = (q * QK_SCALE).astype(BF16)
    kf_ref[...] = k
    vf_ref[...] = v
    kb_ref[...] = k.astype(BF16)
    vb_ref[...] = v.astype(BF16)
    v_n = _ln(v_a, lng_ref[...], lnb_ref[...])
    vn_ref[...] = v_n
    _sgu(u, v_n.astype(BF16), ws_ref, bs_ref, oa_ref, tm, blk)


def _proj_prompt_kernel(h_ref, w_ref, lng_ref, lnb_ref, ws_ref, bs_ref, kbias_ref,
                        k_all_ref, v_all_ref,
                        oa_ref, qt1_ref, qt2_ref, ka1_ref, ka2_ref, vt_ref, kf_ref, vf_ref,
                        *, tm, blk):
    hb = h_ref[...].astype(BF16)

    def proj(off, width):
        return jnp.dot(hb, w_ref[:, off:off + width], preferred_element_type=F32)

    u = proj(0, D_GMLP)
    v_a = proj(D_GMLP, D_GMLP)
    q = proj(OFF_Q, 512)
    k = proj(OFF_K, 512)
    v = proj(OFF_V, D_DIFF)
    for hd in range(N_DIFF_HEADS):
        lanes = slice(hd * DIFF_V_DIM, (hd + 1) * DIFF_V_DIM)
        kf_ref[:, hd, :] = k[:, lanes]
        vf_ref[:, hd, :] = v[:, lanes]
    vt_ref[...] = v.T.astype(BF16)

    lane = lax.broadcasted_iota(jnp.int32, k.shape, 1) % DIFF_V_DIM
    kbias = kbias_ref[...]
    ka1_ref[...] = jnp.where(lane < DIFF_QK_DIM, k, kbias).astype(BF16)
    ka2_ref[...] = jnp.where(lane >= DIFF_QK_DIM, k, kbias).astype(BF16)

    qt = (q * (QK_SCALE * LOG2E)).T
    row = lax.broadcasted_iota(jnp.int32, qt.shape, 0) % DIFF_V_DIM
    ones1 = jnp.where((row >= DIFF_QK_DIM) & (row < DIFF_QK_DIM + N_BIAS), 1.0, 0.0)
    ones2 = jnp.where(row < N_BIAS, 1.0, 0.0)
    qt1_ref[...] = jnp.where(row < DIFF_QK_DIM, qt, ones1).astype(BF16)
    qt2_ref[...] = jnp.where(row >= DIFF_QK_DIM, qt, ones2).astype(BF16)

    v_n = _ln(v_a, lng_ref[...], lnb_ref[...])
    _sgu(u, v_n.astype(BF16), ws_ref, bs_ref, oa_ref, tm, blk)


def _proj_in_specs(tm, layer):
    row = lambda i: (i, 0)
    per_layer = lambda i: (layer, 0, 0)
    return [pl.BlockSpec((tm, D_MODEL), row),
            pl.BlockSpec((None, D_MODEL, D_IN), per_layer),
            pl.BlockSpec((None, 1, D_GMLP), per_layer),
            pl.BlockSpec((None, 1, D_GMLP), per_layer),
            pl.BlockSpec((None, N_GMLP_HEADS, GMLP_BLOCK, GMLP_BLOCK), lambda i: (layer, 0, 0, 0)),
            pl.BlockSpec((None, GMLP_BLOCK, N_GMLP_HEADS), per_layer)]


def _proj_sample(h, w_in_b, ln_g, ln_b, w_s, b_s_t, layer, tm, blk):
    t = h.shape[0]
    row = lambda i: (i, 0)
    return pl.pallas_call(
        functools.partial(_proj_sample_kernel, tm=tm, blk=blk),
        grid=(t // tm,),
        in_specs=_proj_in_specs(tm, layer),
        out_specs=[pl.BlockSpec((tm, 512), row)] * 7,
        out_shape=[jax.ShapeDtypeStruct((t, 512), BF16),
                   jax.ShapeDtypeStruct((t, 512), BF16),
                   jax.ShapeDtypeStruct((t, 512), F32),
                   jax.ShapeDtypeStruct((t, 512), F32),
                   jax.ShapeDtypeStruct((t, 512), BF16),
                   jax.ShapeDtypeStruct((t, 512), BF16),
                   jax.ShapeDtypeStruct((t, 512), F32)],
        compiler_params=_params(("parallel",)),
        name="proj_gmlp_sample",
    )(h, w_in_b, ln_g, ln_b, w_s, b_s_t)


def _proj_prompt(h, w_in_b, ln_g, ln_b, w_s, b_s_t, kbias, k_all, v_all, layer, tm):
    t = h.shape[0]
    row = lambda i: (i, 0)
    col = lambda i: (0, i)
    rows_bf = jax.ShapeDtypeStruct((t, 512), BF16)
    cols_bf = jax.ShapeDtypeStruct((512, t), BF16)
    heads_f = jax.ShapeDtypeStruct(k_all.shape, F32)
    heads_spec = pl.BlockSpec((None, None, tm, N_DIFF_HEADS, DIFF_V_DIM),
                              lambda i: (layer, 0, i, 0, 0))
    in_specs = _proj_in_specs(tm, layer) + [pl.BlockSpec((tm, 512), lambda i: (0, 0)),
                                            pl.BlockSpec(memory_space=pl.ANY),
                                            pl.BlockSpec(memory_space=pl.ANY)]
    return pl.pallas_call(
        functools.partial(_proj_prompt_kernel, tm=tm, blk=GMLP_BLOCK),
        grid=(t // tm,),
        in_specs=in_specs,
        input_output_aliases={len(in_specs) - 2: 6, len(in_specs) - 1: 7},
        out_specs=[pl.BlockSpec((tm, 512), row),
                   pl.BlockSpec((512, tm), col),
                   pl.BlockSpec((512, tm), col),
                   pl.BlockSpec((tm, 512), row),
                   pl.BlockSpec((tm, 512), row),
                   pl.BlockSpec((512, tm), col),
                   heads_spec,
                   heads_spec],
        out_shape=[rows_bf, cols_bf, cols_bf, rows_bf, rows_bf, cols_bf, heads_f, heads_f],
        compiler_params=_params(("parallel",)),
        name="proj_gmlp_prompt",
    )(h, w_in_b, ln_g, ln_b, w_s, b_s_t, kbias, k_all, v_all)


def _lambda(lamp_ref, lam_init):
    lp = lamp_ref[...]
    d1 = jnp.sum(lp[0:1] * lp[1:2], axis=-1, keepdims=True)
    d2 = jnp.sum(lp[2:3] * lp[3:4], axis=-1, keepdims=True)
    return jnp.exp(d1) - jnp.exp(d2) + lam_init


def _head_slope(h):
    return jnp.exp2(-(8.0 / N_DIFF_HEADS) * (h + 1).astype(F32))


def _split_q(q):
    lane = lax.broadcasted_iota(jnp.int32, q.shape, 1)
    zero = jnp.zeros_like(q)
    return jnp.where(lane < DIFF_QK_DIM, q, zero), jnp.where(lane >= DIFF_QK_DIM, q, zero)


def _qk(qz, kc):
    return lax.dot_general(qz, kc, (((1,), (1,)), ((), ())), preferred_element_type=F32)


def _diff_post(o1, o2, lam, g, lam_init):
    o = o1 - lam * o2
    o = o * lax.rsqrt(jnp.mean(o * o, -1, keepdims=True) + RMS_EPS) * g
    return o * (1.0 - lam_init)


class _MapState:
    def __init__(self, qt, ka, m, acc, s_bufs, p_bufs, shift, alpha_s, alpha_p):
        self.qt, self.ka, self.m, self.acc = qt, ka, m, acc
        self.s_bufs, self.p_bufs = s_bufs, p_bufs
        self.shift, self.alpha_s, self.alpha_p = shift, alpha_s, alpha_p


def _attn_prompt_kernel(qt1_ref, qt2_ref, ka1_ref, ka2_ref, vt_ref, diag_ref, lamp_ref, g_ref,
                        o_ref, *scratch, tq, tk, lam_init):
    h = pl.program_id(0)
    i = pl.program_id(1)
    chunk_bias = _head_slope(h) * (LOG2E * tk)
    n_diag = tq // tk
    first_diag = i * n_diag
    per_map = len(scratch) // 2
    maps = []
    for n, (qt, ka) in enumerate(((qt1_ref, ka1_ref), (qt2_ref, ka2_ref))):
        m, acc, s0, s1, p0, p1, shift, alpha_s, alpha_p = scratch[n * per_map:(n + 1) * per_map]
        maps.append(_MapState(qt, ka, m, acc, (s0, s1), (p0, p1), shift, alpha_s, alpha_p))
        m[...] = jnp.full(m.shape, -jnp.inf, F32)
        acc[...] = jnp.zeros(acc.shape, F32)

    def chunk_of(idx):
        j = jnp.where(idx < n_diag, first_diag + idx, idx - n_diag)
        return j, chunk_bias * j.astype(F32)

    def scores(idx, slot, diagonal=None):
        j, c_j = chunk_of(idx)
        start = pl.multiple_of(j * tk, tk)
        for st in maps:
            s = jnp.dot(st.ka[pl.ds(start, tk), :], st.qt[...],
                        preferred_element_type=F32)
            if diagonal is not None:
                s = s + diag_ref[diagonal * tk:(diagonal + 1) * tk, :]
            st.s_bufs[slot][...] = s
            m_prev = st.m[...]
            m_new = jnp.maximum(m_prev, jnp.max(s, axis=0, keepdims=True) + c_j)
            st.alpha_s[slot:slot + 1, :] = jnp.exp2(m_prev - m_new)
            st.shift[slot:slot + 1, :] = m_new - c_j
            st.m[...] = m_new

    def exponentials(slot):
        for st in maps:
            alpha = st.alpha_s[slot:slot + 1, :]
            x = st.s_bufs[slot][...] - st.shift[slot:slot + 1, :]
            st.p_bufs[slot][...] = jnp.exp2(x.astype(BF16))
            st.alpha_p[slot:slot + 1, :] = alpha

    def values(idx, slot):
        j, _ = chunk_of(idx)
        start = pl.multiple_of(j * tk, tk)
        vtc = jnp.concatenate([vt_ref[:, pl.ds(start, tk)], ones_rows], axis=0)
        for st in maps:
            st.acc[...] = (st.alpha_p[slot:slot + 1, :] * st.acc[...]
                           + jnp.dot(vtc, st.p_bufs[slot][...], preferred_element_type=F32))

    ones_rows = jnp.where(lax.broadcasted_iota(jnp.int32, (SUM_ROWS, tk), 0) == 0,
                          1.0, 0.0).astype(BF16)
    scores(0, 0, diagonal=0)
    scores(1, 1, diagonal=1)
    exponentials(0)

    def body(pair, carry):
        k = 2 * pair
        scores(k + 2, 0)
        exponentials(1)
        values(k, 0)
        scores(k + 3, 1)
        exponentials(0)
        values(k + 1, 1)
        return carry

    lax.fori_loop(0, i, body, 0)
    last = 2 * i
    exponentials(1)
    values(last, 0)
    values(last + 1, 1)

    lam = _lambda(lamp_ref, lam_init)
    o1, o2 = (st.acc[:DIFF_V_DIM, :] / st.acc[DIFF_V_DIM:DIFF_V_DIM + 1, :] for st in maps)
    o = o1 - lam * o2
    o = o * lax.rsqrt(jnp.mean(o * o, axis=0, keepdims=True) + RMS_EPS) * g_ref[...]
    o_ref[...] = (o * (1.0 - lam_init)).T.astype(o_ref.dtype)


def _attn_prompt(qt1, qt2, ka1, ka2, vt, diag, lamp, g_col, lam_init, tq, tk):
    s = ka1.shape[0]
    assert tq == 2 * tk
    once = pl.Buffered(1)
    q_spec = pl.BlockSpec((DIFF_V_DIM, tq), lambda h, i: (h, i))
    k_spec = pl.BlockSpec((s, DIFF_V_DIM), lambda h, i: (0, h), pipeline_mode=once)
    row = pltpu.VMEM((1, tq), F32)
    two_rows = pltpu.VMEM((2, tq), F32)
    score = pltpu.VMEM((tk, tq), F32)
    prob = pltpu.VMEM((tk, tq), BF16)
    map_scratch = [row, pltpu.VMEM((DIFF_V_DIM + SUM_ROWS, tq), F32), score, score, prob, prob,
                   two_rows, two_rows, two_rows]
    return pl.pallas_call(
        functools.partial(_attn_prompt_kernel, tq=tq, tk=tk, lam_init=lam_init),
        grid=(N_DIFF_HEADS, s // tq),
        in_specs=[q_spec, q_spec, k_spec, k_spec,
                  pl.BlockSpec((DIFF_V_DIM, s), lambda h, i: (h, 0), pipeline_mode=once),
                  pl.BlockSpec((None, tq, tq), lambda h, i: (h, 0, 0), pipeline_mode=once),
                  pl.BlockSpec((4, DIFF_QK_DIM), lambda h, i: (0, 0)),
                  pl.BlockSpec((DIFF_V_DIM, 1), lambda h, i: (0, 0))],
        out_specs=pl.BlockSpec((tq, DIFF_V_DIM), lambda h, i: (i, h)),
        out_shape=jax.ShapeDtypeStruct((s, D_DIFF), BF16),
        scratch_shapes=map_scratch * 2,
        compiler_params=_params(("parallel", "arbitrary")),
        name="attn_prompt",
    )(qt1, qt2, ka1, ka2, vt, diag, lamp, g_col)


def _bf16_terms(x, n):
    terms = []
    rem = x
    for _ in range(n):
        part = lax.bitcast_convert_type(
            lax.bitcast_convert_type(rem, jnp.uint32) & jnp.uint32(0xFFFF0000), F32)
        terms.append(part)
        rem = rem - part
    return terms


def _alibi_tables(tk, tq):
    slopes = jnp.asarray([2.0 ** (-8.0 * (h + 1) / N_DIFF_HEADS) for h in range(N_DIFF_HEADS)], F32)
    pos = jnp.arange(tk, dtype=F32)
    f = (slopes[:, None] * LOG2E) * pos[None, :]
    terms = jnp.stack(_bf16_terms(f, N_BIAS), axis=-1)
    head = jnp.zeros((N_DIFF_HEADS, tk, DIFF_V_DIM), F32)
    head = head.at[:, :, DIFF_QK_DIM:DIFF_QK_DIM + N_BIAS].set(terms)
    head = head.at[:, :, :N_BIAS].set(terms)
    kbias = head.transpose(1, 0, 2).reshape(tk, N_DIFF_HEADS * DIFF_V_DIM)
    s_i = jnp.arange(tq)[:, None]
    t_i = jnp.arange(tq)[None, :]
    ahead = jnp.maximum(s_i - t_i, 0).astype(F32)
    corr = (-2.0 * LOG2E) * slopes[:, None, None] * ahead[None]
    diag = jnp.where(((s_i // CHUNK) <= (t_i // CHUNK))[None], corr, -jnp.inf)
    return kbias, diag


def _attn_sample_kernel(q_ref, kn_ref, vn_ref, ck_ref, cv_ref, lamp_ref, g_ref, o_ref,
                        *, t, past, lam_init):
    t_pos = past + lax.broadcasted_iota(jnp.int32, (2 * t, 1), 0) % t

    def neg_dist(s_pos):
        vis = (s_pos // CHUNK) <= (t_pos // CHUNK)
        return jnp.where(vis, -jnp.abs(t_pos - s_pos).astype(F32), -jnp.inf)

    nd_c = neg_dist(lax.broadcasted_iota(jnp.int32, (1, past), 1))
    nd_n = neg_dist(past + lax.broadcasted_iota(jnp.int32, (1, t), 1))
    lam = _lambda(lamp_ref, lam_init)
    for hd in range(N_DIFF_HEADS):
        slope = 2.0 ** (-8.0 * (hd + 1) / N_DIFF_HEADS)
        lanes = slice(hd * DIFF_V_DIM, (hd + 1) * DIFF_V_DIM)
        qs = jnp.concatenate(_split_q(q_ref[:, lanes]), axis=0)
        head_rows = pl.ds(hd, past, stride=N_DIFF_HEADS)
        kc = ck_ref[head_rows, :].astype(BF16)
        vc = cv_ref[head_rows, :].astype(BF16)
        sc = _qk(qs, kc) + slope * nd_c
        sn = _qk(qs, kn_ref[:, lanes]) + slope * nd_n
        m = jnp.maximum(jnp.max(sc, axis=-1, keepdims=True), jnp.max(sn, axis=-1, keepdims=True))
        pc = jnp.exp(sc - m)
        pn = jnp.exp(sn - m)
        l = jnp.sum(pc, axis=-1, keepdims=True) + jnp.sum(pn, axis=-1, keepdims=True)
        o = (jnp.dot(pc.astype(BF16), vc, preferred_element_type=F32)
             + jnp.dot(pn.astype(BF16), vn_ref[:, lanes], preferred_element_type=F32)) / l
        o_ref[:, lanes] = _diff_post(o[:t], o[t:], lam, g_ref[...], lam_init).astype(o_ref.dtype)


def _attn_sample(q, kb, vb, cache_k, cache_v, layer, lamp, g, lam_init, nb, t):
    past = cache_k.shape[2] // N_DIFF_HEADS
    tok = pl.BlockSpec((t, D_DIFF), lambda b: (b, 0))
    cache = pl.BlockSpec((None, None, past * N_DIFF_HEADS, DIFF_V_DIM),
                         lambda b: (layer, b, 0, 0))
    return pl.pallas_call(
        functools.partial(_attn_sample_kernel, t=t, past=past, lam_init=lam_init),
        grid=(nb,),
        in_specs=[tok, tok, tok, cache, cache,
                  pl.BlockSpec((4, DIFF_QK_DIM), lambda b: (0, 0)),
                  pl.BlockSpec((1, DIFF_V_DIM), lambda b: (0, 0))],
        out_specs=tok,
        out_shape=jax.ShapeDtypeStruct((nb * t, D_DIFF), BF16),
        compiler_params=_params(("parallel",)),
        name="attn_sample",
    )(q, kb, vb, cache_k, cache_v, lamp, g)


def _split_bf16(x):
    hi = x.astype(BF16)
    lo = (x - hi.astype(F32)).astype(BF16)
    return hi, lo


def _top2_sum(a, b, c, d):
    return jnp.maximum(jnp.maximum(jnp.maximum(a + b, a + c), jnp.maximum(a + d, b + c)),
                       jnp.maximum(b + d, c + d))


def _gates_t(aff, sel):
    rows = [sel[e:e + 1, :] for e in range(N_EXPERTS)]
    arow = [aff[e:e + 1, :] for e in range(N_EXPERTS)]
    gs = [_top2_sum(*rows[4 * g:4 * g + 4]) for g in range(N_GROUPS)]
    best = jnp.maximum(jnp.maximum(gs[0], gs[1]), jnp.maximum(gs[2], gs[3]))
    is_g = []
    taken = None
    for g in range(N_GROUPS):
        hit = gs[g] == best
        if taken is None:
            is_g.append(hit)
            taken = hit
        else:
            is_g.append(hit & ~taken)
            taken = taken | hit
    vals, affs = [], []
    for i in range(EXPERTS_PER_GROUP):
        v = rows[i]
        a = arow[i]
        for g in range(1, N_GROUPS):
            v = jnp.where(is_g[g], rows[4 * g + i], v)
            a = jnp.where(is_g[g], arow[4 * g + i], a)
        vals.append(v)
        affs.append(a)
    picked = []
    for i in range(EXPERTS_PER_GROUP):
        rank = jnp.zeros_like(vals[i])
        for j in range(EXPERTS_PER_GROUP):
            if j == i:
                continue
            ahead = (vals[j] >= vals[i]) if j < i else (vals[j] > vals[i])
            rank = rank + jnp.where(ahead, 1.0, 0.0)
        picked.append(rank < 2.0)
    wsum = jnp.zeros_like(affs[0])
    for i in range(EXPERTS_PER_GROUP):
        wsum = wsum + jnp.where(picked[i], affs[i], 0.0)
    row_id = lax.broadcasted_iota(jnp.int32, aff.shape, 0)
    gates = jnp.zeros(aff.shape, F32)
    for g in range(N_GROUPS):
        for i in range(EXPERTS_PER_GROUP):
            w = jnp.where(is_g[g] & picked[i], affs[i] / wsum, 0.0)
            gates = jnp.where(row_id == 4 * g + i, w, gates)
    return gates


def _mix_kernel(h_ref, oa_ref, ob_ref, wo_ref, bo_ref, g_ref, b_ref, wr_hi_ref, wr_lo_ref,
                rb_ref, h1_ref, h1b_ref, gates_ref, *, tm, sub):
    for r0 in range(0, tm, sub):
        rows = slice(r0, r0 + sub)
        mix = (jnp.dot(oa_ref[rows, :], wo_ref[:D_GMLP, :], preferred_element_type=F32)
               + jnp.dot(ob_ref[rows, :], wo_ref[D_GMLP:, :], preferred_element_type=F32)
               + bo_ref[...])
        h1 = _ln(ALPHA * h_ref[rows, :] + mix, g_ref[...], b_ref[...])
        h1_ref[rows, :] = h1
        h1b_ref[rows, :] = h1.astype(BF16)
        hi, lo = _split_bf16(h1)
        logits = (jnp.dot(hi, wr_hi_ref[...], preferred_element_type=F32)
                  + jnp.dot(lo, wr_hi_ref[...], preferred_element_type=F32)
                  + jnp.dot(hi, wr_lo_ref[...], preferred_element_type=F32))
        lt = logits.T[:N_EXPERTS, :]
        aff = 1.0 / (1.0 + jnp.exp(-lt))
        sel = aff + rb_ref[...]
        gates = _gates_t(aff, sel)
        gpad = jnp.concatenate([gates, jnp.zeros((LANES - N_EXPERTS, sub), F32)], axis=0)
        gates_ref[rows, :] = gpad.T


def _mix(h, o_a, o_b, w_out_b, layer, b_out, ln_g, ln_b, wr_hi, wr_lo, rb, tm):
    t = h.shape[0]
    row = lambda i: (i, 0)
    const2 = lambda i: (0, 0)
    return pl.pallas_call(
        functools.partial(_mix_kernel, tm=tm, sub=MIX_SUB_ROWS),
        grid=(t // tm,),
        in_specs=[pl.BlockSpec((tm, D_MODEL), row),
                  pl.BlockSpec((tm, D_GMLP), row),
                  pl.BlockSpec((tm, D_DIFF), row),
                  pl.BlockSpec((None, D_MODEL, D_MODEL), lambda i: (layer, 0, 0)),
                  pl.BlockSpec((1, D_MODEL), const2),
                  pl.BlockSpec((1, D_MODEL), const2),
                  pl.BlockSpec((1, D_MODEL), const2),
                  pl.BlockSpec((D_MODEL, LANES), const2),
                  pl.BlockSpec((D_MODEL, LANES), const2),
                  pl.BlockSpec((N_EXPERTS, 1), const2)],
        out_specs=[pl.BlockSpec((tm, D_MODEL), row),
                   pl.BlockSpec((tm, D_MODEL), row),
                   pl.BlockSpec((tm, LANES), row)],
        out_shape=[jax.ShapeDtypeStruct((t, D_MODEL), F32),
                   jax.ShapeDtypeStruct((t, D_MODEL), BF16),
                   jax.ShapeDtypeStruct((t, LANES), F32)],
        compiler_params=_params(("parallel",)),
        name="mix_router",
    )(h, o_a, o_b, w_out_b, b_out.reshape(1, -1), ln_g.reshape(1, -1), ln_b.reshape(1, -1),
      wr_hi, wr_lo, rb)


MIX_SUB_ROWS = 256
EXPERTS_PER_STEP = 2


def _gelu_gated(x, half_gate):
    c0 = math.sqrt(2.0 / math.pi)
    inner = x * (c0 + (c0 * 0.044715) * (x * x))
    return (x * (1.0 + jnp.tanh(inner))) * half_gate


def _moe_kernel(x_ref, gates_ref, h1_ref, *refs):
    wu_refs = refs[:EXPERTS_PER_STEP]
    bu_ref, wd_ref, bd_ref, g_ref, b_ref, o_ref, acc_ref = refs[EXPERTS_PER_STEP:]
    step = pl.program_id(1)

    @pl.when(step == 0)
    def _():
        acc_ref[...] = jnp.zeros(acc_ref.shape, F32)

    gates = gates_ref[...]
    lane = lax.broadcasted_iota(jnp.int32, gates.shape, 1)
    x = x_ref[...]
    down = None
    for n, wu_ref in enumerate(wu_refs):
        e = EXPERTS_PER_STEP * step + n
        half_gate = 0.5 * jnp.sum(jnp.where(lane == e, gates, 0.0), axis=-1, keepdims=True)
        cols = slice(n * D_EXPERT, (n + 1) * D_EXPERT)
        up = jnp.dot(x, wu_ref[...], preferred_element_type=F32) + bu_ref[:, cols]
        hid = _gelu_gated(up, half_gate).astype(BF16)
        part = jnp.dot(hid, wd_ref[cols, :], preferred_element_type=F32)
        down = part if down is None else down + part
    acc_ref[...] += down

    @pl.when(step == N_EXPERTS // EXPERTS_PER_STEP - 1)
    def _():
        f = acc_ref[...] + jnp.dot(gates.astype(BF16), bd_ref[...], preferred_element_type=F32)
        o_ref[...] = _ln(ALPHA * h1_ref[...] + f, g_ref[...], b_ref[...])


def _moe(h1, h1b, gates, w_up_b, b_up, w_down_b, b_down_pad, layer, ln_g, ln_b, tm):
    t = h1.shape[0]
    n_steps = N_EXPERTS // EXPERTS_PER_STEP
    row = lambda i, p: (i, 0)
    const2 = lambda i, p: (0, 0)
    up_spec = lambda n: pl.BlockSpec((None, None, D_MODEL, D_EXPERT),
                                     lambda i, p: (layer, EXPERTS_PER_STEP * p + n, 0, 0))
    return pl.pallas_call(
        _moe_kernel,
        grid=(t // tm, n_steps),
        in_specs=[pl.BlockSpec((tm, D_MODEL), row),
                  pl.BlockSpec((tm, LANES), row),
                  pl.BlockSpec((tm, D_MODEL), row),
                  *[up_spec(n) for n in range(EXPERTS_PER_STEP)],
                  pl.BlockSpec((None, None, 1, EXPERTS_PER_STEP * D_EXPERT),
                               lambda i, p: (layer, p, 0, 0)),
                  pl.BlockSpec((None, None, EXPERTS_PER_STEP * D_EXPERT, D_MODEL),
                               lambda i, p: (layer, p, 0, 0)),
                  pl.BlockSpec((None, LANES, D_MODEL), lambda i, p: (layer, 0, 0)),
                  pl.BlockSpec((1, D_MODEL), const2),
                  pl.BlockSpec((1, D_MODEL), const2)],
        out_specs=pl.BlockSpec((tm, D_MODEL), row),
        out_shape=jax.ShapeDtypeStruct((t, D_MODEL), F32),
        scratch_shapes=[pltpu.VMEM((tm, D_MODEL), F32)],
        compiler_params=_params(("parallel", "arbitrary")),
        name="moe_ln",
    )(h1b, gates, h1, *[w_up_b] * EXPERTS_PER_STEP, b_up, w_down_b, b_down_pad,
      ln_g.reshape(1, -1), ln_b.reshape(1, -1))


def kernel(x_prompt, x_sample, cache_k, cache_v, ln_in_g, ln_in_b, w_in, w_out, b_out,
           gmlp_ln_g, gmlp_ln_b, gmlp_w_s, gmlp_b_s, lambda_q1, lambda_k1, lambda_q2,
           lambda_k2, subln_g, ln_mix_g, ln_mix_b, w_router, router_bias, w_up, b_up,
           w_down, b_down, ln_ffn_g, ln_ffn_b):
    nbp, seq, d = x_prompt.shape
    nbs, tdec, _ = x_sample.shape
    assert nbp == 1 and d == D_MODEL
    ts = nbs * tdec
    past = cache_k.shape[2]
    ck_rows = cache_k.reshape(DEPTH, nbs, past * N_DIFF_HEADS, DIFF_V_DIM)
    cv_rows = cache_v.reshape(DEPTH, nbs, past * N_DIFF_HEADS, DIFF_V_DIM)

    hp = _input_ln(x_prompt.reshape(seq, d), ln_in_g, ln_in_b, 1024)
    hs = _input_ln(x_sample.reshape(ts, d), ln_in_g, ln_in_b, ts)

    wr_pad = jnp.pad(w_router, ((0, 0), (0, LANES - N_EXPERTS)))
    wr_hi, wr_lo = (w.astype(BF16) for w in _bf16_terms(wr_pad, 2))
    rb = router_bias.astype(F32).reshape(N_EXPERTS, 1)
    kbias, diag = _alibi_tables(ATTN_TK, ATTN_TQ)

    w_in_b = w_in.astype(BF16)
    w_out_b = w_out.astype(BF16)
    w_up_b = w_up.astype(BF16)
    steps = N_EXPERTS // EXPERTS_PER_STEP
    w_down_b = w_down.astype(BF16).reshape(DEPTH, steps, EXPERTS_PER_STEP * D_EXPERT, D_MODEL)
    b_up4 = b_up.reshape(DEPTH, steps, 1, EXPERTS_PER_STEP * D_EXPERT)
    b_down_pad = jnp.pad(b_down, ((0, 0), (0, LANES - N_EXPERTS), (0, 0))).astype(BF16)
    gln_g = gmlp_ln_g.reshape(DEPTH, 1, D_GMLP)
    gln_b = gmlp_ln_b.reshape(DEPTH, 1, D_GMLP)
    b_s_t = gmlp_b_s.transpose(0, 2, 1)

    k_all = jnp.zeros((DEPTH, 1, seq, N_DIFF_HEADS, DIFF_V_DIM), F32)
    v_all = jnp.zeros((DEPTH, 1, seq, N_DIFF_HEADS, DIFF_V_DIM), F32)
    ksm, vsm, gvs = [], [], []
    for l in range(DEPTH):
        lam_init = 0.8 - 0.6 * math.exp(-0.3 * l)
        lamp = jnp.stack([lambda_q1[l], lambda_k1[l], lambda_q2[l], lambda_k2[l]]).astype(F32)

        def tail(h, o_a, o_b, tm_mix, tm_moe):
            h1, h1b, gates = _mix(h, o_a, o_b, w_out_b, l, b_out[l], ln_mix_g[l], ln_mix_b[l],
                                  wr_hi, wr_lo, rb, tm_mix)
            return _moe(h1, h1b, gates, w_up_b, b_up4, w_down_b, b_down_pad, l,
                        ln_ffn_g[l], ln_ffn_b[l], tm_moe)

        o_a, qt1, qt2, ka1, ka2, vt, k_all, v_all = _proj_prompt(
            hp, w_in_b, gln_g, gln_b, gmlp_w_s, b_s_t, kbias, k_all, v_all, l, ATTN_TK)
        o_b = _attn_prompt(qt1, qt2, ka1, ka2, vt, diag, lamp,
                           subln_g[l].reshape(DIFF_V_DIM, 1), lam_init, ATTN_TQ, ATTN_TK)
        hp = tail(hp, o_a, o_b, 1024, 1024)

        o_a, q, kf, vf, kb, vb, vn = _proj_sample(hs, w_in_b, gln_g, gln_b, gmlp_w_s, b_s_t,
                                                  l, ts, tdec)
        o_b = _attn_sample(q, kb, vb, ck_rows, cv_rows, l, lamp,
                           subln_g[l].reshape(1, DIFF_V_DIM), lam_init, nbs, tdec)
        hs = tail(hs, o_a, o_b, ts, ts)
        ksm.append(kf)
        vsm.append(vf)
        gvs.append(vn)

    hd = (N_DIFF_HEADS, DIFF_V_DIM)
    return (hp.reshape(1, seq, d),
            hs.reshape(nbs, tdec, d),
            k_all,
            v_all,
            jnp.stack(ksm).reshape(DEPTH, nbs, tdec, *hd),
            jnp.stack(vsm).reshape(DEPTH, nbs, tdec, *hd),
            jnp.stack(gvs).reshape(DEPTH, nbs, tdec, D_GMLP))
```

```python
import functools
import math

import jax
import jax.numpy as jnp
from jax import lax
from jax.experimental import pallas as pl
from jax.experimental.pallas import tpu as pltpu

F32 = jnp.float32
BF16 = jnp.bfloat16

D_MODEL = 1024
DEPTH = 4
CHUNK = 64
D_GMLP = 512
GMLP_BLOCK = 128
GMLP_HEAD = 128
N_GMLP_HEADS = 4
D_DIFF = 512
N_DIFF_HEADS = 4
DIFF_V_DIM = 128
DIFF_QK_DIM = 64
OFF_Q = 2 * D_GMLP
OFF_K = OFF_Q + 512
OFF_V = OFF_K + 512
D_IN = OFF_V + D_DIFF
N_EXPERTS = 16
N_GROUPS = 4
EXPERTS_PER_GROUP = 4
D_EXPERT = 512
ALPHA = (2.0 * DEPTH) ** 0.25
LN_EPS = 1e-5
RMS_EPS = 1e-5
QK_SCALE = DIFF_QK_DIM ** -0.5
LOG2E = math.log2(math.e)
ATTN_TK = 512
ATTN_TQ = 1024
SUM_ROWS = 16
N_BIAS = 3
LANES = 128
VMEM_LIMIT = 48 * 1024 * 1024


def _ln(x, g, b):
    mu = jnp.mean(x, -1, keepdims=True)
    xc = x - mu
    var = jnp.mean(xc * xc, -1, keepdims=True)
    return xc * lax.rsqrt(var + LN_EPS) * g + b


def _params(sem):
    return pltpu.CompilerParams(dimension_semantics=sem, vmem_limit_bytes=VMEM_LIMIT)


def _ln_kernel(x_ref, g_ref, b_ref, o_ref):
    o_ref[...] = _ln(x_ref[...], g_ref[...], b_ref[...])


def _input_ln(x, g, b, tm):
    t, d = x.shape
    return pl.pallas_call(
        _ln_kernel,
        grid=(t // tm,),
        in_specs=[pl.BlockSpec((tm, d), lambda i: (i, 0)),
                  pl.BlockSpec((1, d), lambda i: (0, 0)),
                  pl.BlockSpec((1, d), lambda i: (0, 0))],
        out_specs=pl.BlockSpec((tm, d), lambda i: (i, 0)),
        out_shape=jax.ShapeDtypeStruct((t, d), F32),
        compiler_params=_params(("parallel",)),
        name="input_ln",
    )(x, g.reshape(1, d), b.reshape(1, d))


def _sgu(u, v_nb, ws_ref, bs_ref, oa_ref, tm, blk):
    ri = lax.broadcasted_iota(jnp.int32, (blk, blk), 0) // CHUNK
    ci = lax.broadcasted_iota(jnp.int32, (blk, blk), 1) // CHUNK
    vis = ci <= ri
    for g in range(N_GMLP_HEADS):
        w_g = jnp.where(vis, ws_ref[g, :blk, :blk], 0.0).astype(BF16)
        b_g = bs_ref[:blk, g:g + 1]
        c0 = g * GMLP_HEAD
        for r in range(tm // blk):
            r0 = r * blk
            s = jnp.dot(w_g, v_nb[r0:r0 + blk, c0:c0 + GMLP_HEAD],
                        preferred_element_type=F32) + b_g
            oa_ref[r0:r0 + blk, c0:c0 + GMLP_HEAD] = (
                u[r0:r0 + blk, c0:c0 + GMLP_HEAD] * s).astype(BF16)


def _proj_sample_kernel(h_ref, w_ref, lng_ref, lnb_ref, ws_ref, bs_ref,
                        oa_ref, q_ref, kf_ref, vf_ref, kb_ref, vb_ref, vn_ref, *, tm, blk):
    hb = h_ref[...].astype(BF16)

    def proj(off, width):
        return jnp.dot(hb, w_ref[:, off:off + width], preferred_element_type=F32)

    u = proj(0, D_GMLP)
    v_a = proj(D_GMLP, D_GMLP)
    q = proj(OFF_Q, 512)
    k = proj(OFF_K, 512)
    v = proj(OFF_V, D_DIFF)
    q_ref[...] = (q * QK_SCALE).astype(BF16)
    kf_ref[...] = k
    vf_ref[...] = v
    kb_ref[...] = k.astype(BF16)
    vb_ref[...] = v.astype(BF16)
    v_n = _ln(v_a, lng_ref[...], lnb_ref[...])
    vn_ref[...] = v_n
    _sgu(u, v_n.astype(BF16), ws_ref, bs_ref, oa_ref, tm, blk)


def _proj_prompt_kernel(h_ref, w_ref, lng_ref, lnb_ref, ws_ref, bs_ref, kbias_ref,
                        k_all_ref, v_all_ref,
                        oa_ref, qt1_ref, qt2_ref, ka1_ref, ka2_ref, vt_ref, kf_ref, vf_ref,
                        *, tm, blk):
    hb = h_ref[...].astype(BF16)

    def proj(off, width):
        return jnp.dot(hb, w_ref[:, off:off + width], preferred_element_type=F32)

    u = proj(0, D_GMLP)
    v_a = proj(D_GMLP, D_GMLP)
    q = proj(OFF_Q, 512)
    k = proj(OFF_K, 512)
    v = proj(OFF_V, D_DIFF)
    for hd in range(N_DIFF_HEADS):
        lanes = slice(hd * DIFF_V_DIM, (hd + 1) * DIFF_V_DIM)
        head_rows = pl.ds(hd, tm, stride=N_DIFF_HEADS)
        kf_ref[head_rows, :] = k[:, lanes]
        vf_ref[head_rows, :] = v[:, lanes]
    vt_ref[...] = v.T.astype(BF16)

    lane = lax.broadcasted_iota(jnp.int32, k.shape, 1) % DIFF_V_DIM
    kbias = kbias_ref[...]
    ka1_ref[...] = jnp.where(lane < DIFF_QK_DIM, k, kbias).astype(BF16)
    ka2_ref[...] = jnp.where(lane >= DIFF_QK_DIM, k, kbias).astype(BF16)

    qt = (q * (QK_SCALE * LOG2E)).T
    row = lax.broadcasted_iota(jnp.int32, qt.shape, 0) % DIFF_V_DIM
    ones1 = jnp.where((row >= DIFF_QK_DIM) & (row < DIFF_QK_DIM + N_BIAS), 1.0, 0.0)
    ones2 = jnp.where(row < N_BIAS, 1.0, 0.0)
    qt1_ref[...] = jnp.where(row < DIFF_QK_DIM, qt, ones1).astype(BF16)
    qt2_ref[...] = jnp.where(row >= DIFF_QK_DIM, qt, ones2).astype(BF16)

    v_n = _ln(v_a, lng_ref[...], lnb_ref[...])
    _sgu(u, v_n.astype(BF16), ws_ref, bs_ref, oa_ref, tm, blk)


def _proj_in_specs(tm, layer):
    row = lambda i: (i, 0)
    per_layer = lambda i: (layer, 0, 0)
    return [pl.BlockSpec((tm, D_MODEL), row),
            pl.BlockSpec((None, D_MODEL, D_IN), per_layer),
            pl.BlockSpec((None, 1, D_GMLP), per_layer),
            pl.BlockSpec((None, 1, D_GMLP), per_layer),
            pl.BlockSpec((None, N_GMLP_HEADS, GMLP_BLOCK, GMLP_BLOCK), lambda i: (layer, 0, 0, 0)),
            pl.BlockSpec((None, GMLP_BLOCK, N_GMLP_HEADS), per_layer)]


def _proj_sample(h, w_in_b, ln_g, ln_b, w_s, b_s_t, layer, tm, blk):
    t = h.shape[0]
    row = lambda i: (i, 0)
    return pl.pallas_call(
        functools.partial(_proj_sample_kernel, tm=tm, blk=blk),
        grid=(t // tm,),
        in_specs=_proj_in_specs(tm, layer),
        out_specs=[pl.BlockSpec((tm, 512), row)] * 7,
        out_shape=[jax.ShapeDtypeStruct((t, 512), BF16),
                   jax.ShapeDtypeStruct((t, 512), BF16),
                   jax.ShapeDtypeStruct((t, 512), F32),
                   jax.ShapeDtypeStruct((t, 512), F32),
                   jax.ShapeDtypeStruct((t, 512), BF16),
                   jax.ShapeDtypeStruct((t, 512), BF16),
                   jax.ShapeDtypeStruct((t, 512), F32)],
        compiler_params=_params(("parallel",)),
        name="proj_gmlp_sample",
    )(h, w_in_b, ln_g, ln_b, w_s, b_s_t)


def _proj_prompt(h, w_in_b, ln_g, ln_b, w_s, b_s_t, kbias, k_all, v_all, layer, tm):
    t = h.shape[0]
    row = lambda i: (i, 0)
    col = lambda i: (0, i)
    rows_bf = jax.ShapeDtypeStruct((t, 512), BF16)
    cols_bf = jax.ShapeDtypeStruct((512, t), BF16)
    heads_f = jax.ShapeDtypeStruct(k_all.shape, F32)
    heads_spec = pl.BlockSpec((None, tm * N_DIFF_HEADS, DIFF_V_DIM), lambda i: (layer, i, 0))
    in_specs = _proj_in_specs(tm, layer) + [pl.BlockSpec((tm, 512), lambda i: (0, 0)),
                                            pl.BlockSpec(memory_space=pl.ANY),
                                            pl.BlockSpec(memory_space=pl.ANY)]
    return pl.pallas_call(
        functools.partial(_proj_prompt_kernel, tm=tm, blk=GMLP_BLOCK),
        grid=(t // tm,),
        in_specs=in_specs,
        input_output_aliases={len(in_specs) - 2: 6, len(in_specs) - 1: 7},
        out_specs=[pl.BlockSpec((tm, 512), row),
                   pl.BlockSpec((512, tm), col),
                   pl.BlockSpec((512, tm), col),
                   pl.BlockSpec((tm, 512), row),
                   pl.BlockSpec((tm, 512), row),
                   pl.BlockSpec((512, tm), col),
                   heads_spec,
                   heads_spec],
        out_shape=[rows_bf, cols_bf, cols_bf, rows_bf, rows_bf, cols_bf, heads_f, heads_f],
        compiler_params=_params(("parallel",)),
        name="proj_gmlp_prompt",
    )(h, w_in_b, ln_g, ln_b, w_s, b_s_t, kbias, k_all, v_all)


def _lambda(lamp_ref, lam_init):
    lp = lamp_ref[...]
    d1 = jnp.sum(lp[0:1] * lp[1:2], axis=-1, keepdims=True)
    d2 = jnp.sum(lp[2:3] * lp[3:4], axis=-1, keepdims=True)
    return jnp.exp(d1) - jnp.exp(d2) + lam_init


def _head_slope(h):
    return jnp.exp2(-(8.0 / N_DIFF_HEADS) * (h + 1).astype(F32))


def _split_q(q):
    lane = lax.broadcasted_iota(jnp.int32, q.shape, 1)
    zero = jnp.zeros_like(q)
    return jnp.where(lane < DIFF_QK_DIM, q, zero), jnp.where(lane >= DIFF_QK_DIM, q, zero)


def _qk(qz, kc):
    return lax.dot_general(qz, kc, (((1,), (1,)), ((), ())), preferred_element_type=F32)


def _diff_post(o1, o2, lam, g, lam_init):
    o = o1 - lam * o2
    o = o * lax.rsqrt(jnp.mean(o * o, -1, keepdims=True) + RMS_EPS) * g
    return o * (1.0 - lam_init)


class _MapState:
    def __init__(self, qt, ka, m, acc, s_bufs, p_bufs, shift, alpha_s, alpha_p):
        self.qt, self.ka, self.m, self.acc = qt, ka, m, acc
        self.s_bufs, self.p_bufs = s_bufs, p_bufs
        self.shift, self.alpha_s, self.alpha_p = shift, alpha_s, alpha_p


def _attn_prompt_kernel(qt1_ref, qt2_ref, ka1_ref, ka2_ref, vt_ref, diag_ref, lamp_ref, g_ref,
                        o_ref, *scratch, tq, tk, lam_init):
    h = pl.program_id(0)
    i = pl.program_id(1)
    chunk_bias = _head_slope(h) * (LOG2E * tk)
    n_diag = tq // tk
    first_diag = i * n_diag
    per_map = len(scratch) // 2
    maps = []
    for n, (qt, ka) in enumerate(((qt1_ref, ka1_ref), (qt2_ref, ka2_ref))):
        m, acc, s0, s1, p0, p1, shift, alpha_s, alpha_p = scratch[n * per_map:(n + 1) * per_map]
        maps.append(_MapState(qt, ka, m, acc, (s0, s1), (p0, p1), shift, alpha_s, alpha_p))
        m[...] = jnp.full(m.shape, -jnp.inf, F32)
        acc[...] = jnp.zeros(acc.shape, F32)

    def chunk_of(idx):
        j = jnp.where(idx < n_diag, first_diag + idx, idx - n_diag)
        return j, chunk_bias * j.astype(F32)

    def scores(idx, slot, diagonal=None):
        j, c_j = chunk_of(idx)
        start = pl.multiple_of(j * tk, tk)
        for st in maps:
            s = jnp.dot(st.ka[pl.ds(start, tk), :], st.qt[...],
                        preferred_element_type=F32)
            if diagonal is not None:
                s = s + diag_ref[diagonal * tk:(diagonal + 1) * tk, :]
            st.s_bufs[slot][...] = s
            m_prev = st.m[...]
            m_new = jnp.maximum(m_prev, jnp.max(s, axis=0, keepdims=True) + c_j)
            st.alpha_s[slot:slot + 1, :] = jnp.exp2(m_prev - m_new)
            st.shift[slot:slot + 1, :] = m_new - c_j
            st.m[...] = m_new

    def exponentials(slot):
        for st in maps:
            alpha = st.alpha_s[slot:slot + 1, :]
            x = st.s_bufs[slot][...] - st.shift[slot:slot + 1, :]
            st.p_bufs[slot][...] = jnp.exp2(x.astype(BF16))
            st.alpha_p[slot:slot + 1, :] = alpha

    def values(idx, slot):
        j, _ = chunk_of(idx)
        start = pl.multiple_of(j * tk, tk)
        vtc = jnp.concatenate([vt_ref[:, pl.ds(start, tk)], ones_rows], axis=0)
        for st in maps:
            st.acc[...] = (st.alpha_p[slot:slot + 1, :] * st.acc[...]
                           + jnp.dot(vtc, st.p_bufs[slot][...], preferred_element_type=F32))

    ones_rows = jnp.where(lax.broadcasted_iota(jnp.int32, (SUM_ROWS, tk), 0) == 0,
                          1.0, 0.0).astype(BF16)
    scores(0, 0, diagonal=0)
    scores(1, 1, diagonal=1)
    exponentials(0)

    def body(pair, carry):
        k = 2 * pair
        scores(k + 2, 0)
        exponentials(1)
        values(k, 0)
        scores(k + 3, 1)
        exponentials(0)
        values(k + 1, 1)
        return carry

    lax.fori_loop(0, i, body, 0)
    last = 2 * i
    exponentials(1)
    values(last, 0)
    values(last + 1, 1)

    lam = _lambda(lamp_ref, lam_init)
    o1, o2 = (st.acc[:DIFF_V_DIM, :] / st.acc[DIFF_V_DIM:DIFF_V_DIM + 1, :] for st in maps)
    o = o1 - lam * o2
    o = o * lax.rsqrt(jnp.mean(o * o, axis=0, keepdims=True) + RMS_EPS) * g_ref[...]
    o_ref[...] = (o * (1.0 - lam_init)).T.astype(o_ref.dtype)


def _attn_prompt(qt1, qt2, ka1, ka2, vt, diag, lamp, g_col, lam_init, tq, tk):
    s = ka1.shape[0]
    assert tq == 2 * tk
    once = pl.Buffered(1)
    q_spec = pl.BlockSpec((DIFF_V_DIM, tq), lambda h, i: (h, i))
    k_spec = pl.BlockSpec((s, DIFF_V_DIM), lambda h, i: (0, h), pipeline_mode=once)
    row = pltpu.VMEM((1, tq), F32)
    two_rows = pltpu.VMEM((2, tq), F32)
    score = pltpu.VMEM((tk, tq), F32)
    prob = pltpu.VMEM((tk, tq), BF16)
    map_scratch = [row, pltpu.VMEM((DIFF_V_DIM + SUM_ROWS, tq), F32), score, score, prob, prob,
                   two_rows, two_rows, two_rows]
    return pl.pallas_call(
        functools.partial(_attn_prompt_kernel, tq=tq, tk=tk, lam_init=lam_init),
        grid=(N_DIFF_HEADS, s // tq),
        in_specs=[q_spec, q_spec, k_spec, k_spec,
                  pl.BlockSpec((DIFF_V_DIM, s), lambda h, i: (h, 0), pipeline_mode=once),
                  pl.BlockSpec((None, tq, tq), lambda h, i: (h, 0, 0), pipeline_mode=once),
                  pl.BlockSpec((4, DIFF_QK_DIM), lambda h, i: (0, 0)),
                  pl.BlockSpec((DIFF_V_DIM, 1), lambda h, i: (0, 0))],
        out_specs=pl.BlockSpec((tq, DIFF_V_DIM), lambda h, i: (i, h)),
        out_shape=jax.ShapeDtypeStruct((s, D_DIFF), BF16),
        scratch_shapes=map_scratch * 2,
        compiler_params=_params(("parallel", "arbitrary")),
        name="attn_prompt",
    )(qt1, qt2, ka1, ka2, vt, diag, lamp, g_col)


def _bf16_terms(x, n):
    terms = []
    rem = x
    for _ in range(n):
        part = lax.bitcast_convert_type(
            lax.bitcast_convert_type(rem, jnp.uint32) & jnp.uint32(0xFFFF0000), F32)
        terms.append(part)
        rem = rem - part
    return terms


def _alibi_tables(tk, tq):
    slopes = jnp.asarray([2.0 ** (-8.0 * (h + 1) / N_DIFF_HEADS) for h in range(N_DIFF_HEADS)], F32)
    pos = jnp.arange(tk, dtype=F32)
    f = (slopes[:, None] * LOG2E) * pos[None, :]
    terms = jnp.stack(_bf16_terms(f, N_BIAS), axis=-1)
    head = jnp.zeros((N_DIFF_HEADS, tk, DIFF_V_DIM), F32)
    head = head.at[:, :, DIFF_QK_DIM:DIFF_QK_DIM + N_BIAS].set(terms)
    head = head.at[:, :, :N_BIAS].set(terms)
    kbias = head.transpose(1, 0, 2).reshape(tk, N_DIFF_HEADS * DIFF_V_DIM)
    s_i = jnp.arange(tq)[:, None]
    t_i = jnp.arange(tq)[None, :]
    ahead = jnp.maximum(s_i - t_i, 0).astype(F32)
    corr = (-2.0 * LOG2E) * slopes[:, None, None] * ahead[None]
    diag = jnp.where(((s_i // CHUNK) <= (t_i // CHUNK))[None], corr, -jnp.inf)
    return kbias, diag


def _attn_sample_kernel(q_ref, kn_ref, vn_ref, ck_ref, cv_ref, lamp_ref, g_ref, o_ref,
                        *, t, past, lam_init):
    t_pos = past + lax.broadcasted_iota(jnp.int32, (2 * t, 1), 0) % t

    def neg_dist(s_pos):
        vis = (s_pos // CHUNK) <= (t_pos // CHUNK)
        return jnp.where(vis, -jnp.abs(t_pos - s_pos).astype(F32), -jnp.inf)

    nd_c = neg_dist(lax.broadcasted_iota(jnp.int32, (1, past), 1))
    nd_n = neg_dist(past + lax.broadcasted_iota(jnp.int32, (1, t), 1))
    lam = _lambda(lamp_ref, lam_init)
    for hd in range(N_DIFF_HEADS):
        slope = 2.0 ** (-8.0 * (hd + 1) / N_DIFF_HEADS)
        lanes = slice(hd * DIFF_V_DIM, (hd + 1) * DIFF_V_DIM)
        qs = jnp.concatenate(_split_q(q_ref[:, lanes]), axis=0)
        head_rows = pl.ds(hd, past, stride=N_DIFF_HEADS)
        kc = ck_ref[head_rows, :].astype(BF16)
        vc = cv_ref[head_rows, :].astype(BF16)
        sc = _qk(qs, kc) + slope * nd_c
        sn = _qk(qs, kn_ref[:, lanes]) + slope * nd_n
        m = jnp.maximum(jnp.max(sc, axis=-1, keepdims=True), jnp.max(sn, axis=-1, keepdims=True))
        pc = jnp.exp(sc - m)
        pn = jnp.exp(sn - m)
        l = jnp.sum(pc, axis=-1, keepdims=True) + jnp.sum(pn, axis=-1, keepdims=True)
        o = (jnp.dot(pc.astype(BF16), vc, preferred_element_type=F32)
             + jnp.dot(pn.astype(BF16), vn_ref[:, lanes], preferred_element_type=F32)) / l
        o_ref[:, lanes] = _diff_post(o[:t], o[t:], lam, g_ref[...], lam_init).astype(o_ref.dtype)


def _attn_sample(q, kb, vb, cache_k, cache_v, layer, lamp, g, lam_init, nb, t):
    past = cache_k.shape[2] // N_DIFF_HEADS
    tok = pl.BlockSpec((t, D_DIFF), lambda b: (b, 0))
    cache = pl.BlockSpec((None, None, past * N_DIFF_HEADS, DIFF_V_DIM),
                         lambda b: (layer, b, 0, 0))
    return pl.pallas_call(
        functools.partial(_attn_sample_kernel, t=t, past=past, lam_init=lam_init),
        grid=(nb,),
        in_specs=[tok, tok, tok, cache, cache,
                  pl.BlockSpec((4, DIFF_QK_DIM), lambda b: (0, 0)),
                  pl.BlockSpec((1, DIFF_V_DIM), lambda b: (0, 0))],
        out_specs=tok,
        out_shape=jax.ShapeDtypeStruct((nb * t, D_DIFF), BF16),
        compiler_params=_params(("parallel",)),
        name="attn_sample",
    )(q, kb, vb, cache_k, cache_v, lamp, g)


def _split_bf16(x):
    hi = x.astype(BF16)
    lo = (x - hi.astype(F32)).astype(BF16)
    return hi, lo


def _top2_sum(a, b, c, d):
    return jnp.maximum(jnp.maximum(jnp.maximum(a + b, a + c), jnp.maximum(a + d, b + c)),
                       jnp.maximum(b + d, c + d))


def _gates_t(aff, sel):
    rows = [sel[e:e + 1, :] for e in range(N_EXPERTS)]
    arow = [aff[e:e + 1, :] for e in range(N_EXPERTS)]
    gs = [_top2_sum(*rows[4 * g:4 * g + 4]) for g in range(N_GROUPS)]
    best = jnp.maximum(jnp.maximum(gs[0], gs[1]), jnp.maximum(gs[2], gs[3]))
    is_g = []
    taken = None
    for g in range(N_GROUPS):
        hit = gs[g] == best
        if taken is None:
            is_g.append(hit)
            taken = hit
        else:
            is_g.append(hit & ~taken)
            taken = taken | hit
    vals, affs = [], []
    for i in range(EXPERTS_PER_GROUP):
        v = rows[i]
        a = arow[i]
        for g in range(1, N_GROUPS):
            v = jnp.where(is_g[g], rows[4 * g + i], v)
            a = jnp.where(is_g[g], arow[4 * g + i], a)
        vals.append(v)
        affs.append(a)
    picked = []
    for i in range(EXPERTS_PER_GROUP):
        rank = jnp.zeros_like(vals[i])
        for j in range(EXPERTS_PER_GROUP):
            if j == i:
                continue
            ahead = (vals[j] >= vals[i]) if j < i else (vals[j] > vals[i])
            rank = rank + jnp.where(ahead, 1.0, 0.0)
        picked.append(rank < 2.0)
    wsum = jnp.zeros_like(affs[0])
    for i in range(EXPERTS_PER_GROUP):
        wsum = wsum + jnp.where(picked[i], affs[i], 0.0)
    row_id = lax.broadcasted_iota(jnp.int32, aff.shape, 0)
    gates = jnp.zeros(aff.shape, F32)
    for g in range(N_GROUPS):
        for i in range(EXPERTS_PER_GROUP):
            w = jnp.where(is_g[g] & picked[i], affs[i] / wsum, 0.0)
            gates = jnp.where(row_id == 4 * g + i, w, gates)
    return gates


def _mix_kernel(h_ref, oa_ref, ob_ref, wo_ref, bo_ref, g_ref, b_ref, wr_hi_ref, wr_lo_ref,
                rb_ref, h1_ref, h1b_ref, gates_ref, *, tm, sub):
    for r0 in range(0, tm, sub):
        rows = slice(r0, r0 + sub)
        mix = (jnp.dot(oa_ref[rows, :], wo_ref[:D_GMLP, :], preferred_element_type=F32)
               + jnp.dot(ob_ref[rows, :], wo_ref[D_GMLP:, :], preferred_element_type=F32)
               + bo_ref[...])
        h1 = _ln(ALPHA * h_ref[rows, :] + mix, g_ref[...], b_ref[...])
        h1_ref[rows, :] = h1
        h1b_ref[rows, :] = h1.astype(BF16)
        hi, lo = _split_bf16(h1)
        logits = (jnp.dot(hi, wr_hi_ref[...], preferred_element_type=F32)
                  + jnp.dot(lo, wr_hi_ref[...], preferred_element_type=F32)
                  + jnp.dot(hi, wr_lo_ref[...], preferred_element_type=F32))
        lt = logits.T[:N_EXPERTS, :]
        aff = 1.0 / (1.0 + jnp.exp(-lt))
        sel = aff + rb_ref[...]
        gates = _gates_t(aff, sel)
        gpad = jnp.concatenate([gates, jnp.zeros((LANES - N_EXPERTS, sub), F32)], axis=0)
        gates_ref[rows, :] = gpad.T


def _mix(h, o_a, o_b, w_out_b, layer, b_out, ln_g, ln_b, wr_hi, wr_lo, rb, tm):
    t = h.shape[0]
    row = lambda i: (i, 0)
    const2 = lambda i: (0, 0)
    return pl.pallas_call(
        functools.partial(_mix_kernel, tm=tm, sub=MIX_SUB_ROWS),
        grid=(t // tm,),
        in_specs=[pl.BlockSpec((tm, D_MODEL), row),
                  pl.BlockSpec((tm, D_GMLP), row),
                  pl.BlockSpec((tm, D_DIFF), row),
                  pl.BlockSpec((None, D_MODEL, D_MODEL), lambda i: (layer, 0, 0)),
                  pl.BlockSpec((1, D_MODEL), const2),
                  pl.BlockSpec((1, D_MODEL), const2),
                  pl.BlockSpec((1, D_MODEL), const2),
                  pl.BlockSpec((D_MODEL, LANES), const2),
                  pl.BlockSpec((D_MODEL, LANES), const2),
                  pl.BlockSpec((N_EXPERTS, 1), const2)],
        out_specs=[pl.BlockSpec((tm, D_MODEL), row),
                   pl.BlockSpec((tm, D_MODEL), row),
                   pl.BlockSpec((tm, LANES), row)],
        out_shape=[jax.ShapeDtypeStruct((t, D_MODEL), F32),
                   jax.ShapeDtypeStruct((t, D_MODEL), BF16),
                   jax.ShapeDtypeStruct((t, LANES), F32)],
        compiler_params=_params(("parallel",)),
        name="mix_router",
    )(h, o_a, o_b, w_out_b, b_out.reshape(1, -1), ln_g.reshape(1, -1), ln_b.reshape(1, -1),
      wr_hi, wr_lo, rb)


MIX_SUB_ROWS = 256
EXPERTS_PER_STEP = 2


def _gelu_gated(x, half_gate):
    c0 = math.sqrt(2.0 / math.pi)
    inner = x * (c0 + (c0 * 0.044715) * (x * x))
    return (x * (1.0 + jnp.tanh(inner))) * half_gate


def _moe_kernel(x_ref, gates_ref, h1_ref, *refs):
    wu_refs = refs[:EXPERTS_PER_STEP]
    bu_ref, wd_ref, bd_ref, g_ref, b_ref, o_ref, acc_ref = refs[EXPERTS_PER_STEP:]
    step = pl.program_id(1)

    @pl.when(step == 0)
    def _():
        acc_ref[...] = jnp.zeros(acc_ref.shape, F32)

    gates = gates_ref[...]
    lane = lax.broadcasted_iota(jnp.int32, gates.shape, 1)
    x = x_ref[...]
    down = None
    for n, wu_ref in enumerate(wu_refs):
        e = EXPERTS_PER_STEP * step + n
        half_gate = 0.5 * jnp.sum(jnp.where(lane == e, gates, 0.0), axis=-1, keepdims=True)
        cols = slice(n * D_EXPERT, (n + 1) * D_EXPERT)
        up = jnp.dot(x, wu_ref[...], preferred_element_type=F32) + bu_ref[:, cols]
        hid = _gelu_gated(up, half_gate).astype(BF16)
        part = jnp.dot(hid, wd_ref[cols, :], preferred_element_type=F32)
        down = part if down is None else down + part
    acc_ref[...] += down

    @pl.when(step == N_EXPERTS // EXPERTS_PER_STEP - 1)
    def _():
        f = acc_ref[...] + jnp.dot(gates.astype(BF16), bd_ref[...], preferred_element_type=F32)
        o_ref[...] = _ln(ALPHA * h1_ref[...] + f, g_ref[...], b_ref[...])


def _moe(h1, h1b, gates, w_up_b, b_up, w_down_b, b_down_pad, layer, ln_g, ln_b, tm):
    t = h1.shape[0]
    n_steps = N_EXPERTS // EXPERTS_PER_STEP
    row = lambda i, p: (i, 0)
    const2 = lambda i, p: (0, 0)
    up_spec = lambda n: pl.BlockSpec((None, None, D_MODEL, D_EXPERT),
                                     lambda i, p: (layer, EXPERTS_PER_STEP * p + n, 0, 0))
    return pl.pallas_call(
        _moe_kernel,
        grid=(t // tm, n_steps),
        in_specs=[pl.BlockSpec((tm, D_MODEL), row),
                  pl.BlockSpec((tm, LANES), row),
                  pl.BlockSpec((tm, D_MODEL), row),
                  *[up_spec(n) for n in range(EXPERTS_PER_STEP)],
                  pl.BlockSpec((None, None, 1, EXPERTS_PER_STEP * D_EXPERT),
                               lambda i, p: (layer, p, 0, 0)),
                  pl.BlockSpec((None, None, EXPERTS_PER_STEP * D_EXPERT, D_MODEL),
                               lambda i, p: (layer, p, 0, 0)),
                  pl.BlockSpec((None, LANES, D_MODEL), lambda i, p: (layer, 0, 0)),
                  pl.BlockSpec((1, D_MODEL), const2),
                  pl.BlockSpec((1, D_MODEL), const2)],
        out_specs=pl.BlockSpec((tm, D_MODEL), row),
        out_shape=jax.ShapeDtypeStruct((t, D_MODEL), F32),
        scratch_shapes=[pltpu.VMEM((tm, D_MODEL), F32)],
        compiler_params=_params(("parallel", "arbitrary")),
        name="moe_ln",
    )(h1b, gates, h1, *[w_up_b] * EXPERTS_PER_STEP, b_up, w_down_b, b_down_pad,
      ln_g.reshape(1, -1), ln_b.reshape(1, -1))


def kernel(x_prompt, x_sample, cache_k, cache_v, ln_in_g, ln_in_b, w_in, w_out, b_out,
           gmlp_ln_g, gmlp_ln_b, gmlp_w_s, gmlp_b_s, lambda_q1, lambda_k1, lambda_q2,
           lambda_k2, subln_g, ln_mix_g, ln_mix_b, w_router, router_bias, w_up, b_up,
           w_down, b_down, ln_ffn_g, ln_ffn_b):
    nbp, seq, d = x_prompt.shape
    nbs, tdec, _ = x_sample.shape
    assert nbp == 1 and d == D_MODEL
    ts = nbs * tdec
    past = cache_k.shape[2]
    ck_rows = cache_k.reshape(DEPTH, nbs, past * N_DIFF_HEADS, DIFF_V_DIM)
    cv_rows = cache_v.reshape(DEPTH, nbs, past * N_DIFF_HEADS, DIFF_V_DIM)

    hp = _input_ln(x_prompt.reshape(seq, d), ln_in_g, ln_in_b, 1024)
    hs = _input_ln(x_sample.reshape(ts, d), ln_in_g, ln_in_b, ts)

    wr_pad = jnp.pad(w_router, ((0, 0), (0, LANES - N_EXPERTS)))
    wr_hi, wr_lo = (w.astype(BF16) for w in _bf16_terms(wr_pad, 2))
    rb = router_bias.astype(F32).reshape(N_EXPERTS, 1)
    kbias, diag = _alibi_tables(ATTN_TK, ATTN_TQ)

    w_in_b = w_in.astype(BF16)
    w_out_b = w_out.astype(BF16)
    w_up_b = w_up.astype(BF16)
    steps = N_EXPERTS // EXPERTS_PER_STEP
    w_down_b = w_down.astype(BF16).reshape(DEPTH, steps, EXPERTS_PER_STEP * D_EXPERT, D_MODEL)
    b_up4 = b_up.reshape(DEPTH, steps, 1, EXPERTS_PER_STEP * D_EXPERT)
    b_down_pad = jnp.pad(b_down, ((0, 0), (0, LANES - N_EXPERTS), (0, 0))).astype(BF16)
    gln_g = gmlp_ln_g.reshape(DEPTH, 1, D_GMLP)
    gln_b = gmlp_ln_b.reshape(DEPTH, 1, D_GMLP)
    b_s_t = gmlp_b_s.transpose(0, 2, 1)

    k_all = jnp.zeros((DEPTH, seq * N_DIFF_HEADS, DIFF_V_DIM), F32)
    v_all = jnp.zeros((DEPTH, seq * N_DIFF_HEADS, DIFF_V_DIM), F32)
    ksm, vsm, gvs = [], [], []
    for l in range(DEPTH):
        lam_init = 0.8 - 0.6 * math.exp(-0.3 * l)
        lamp = jnp.stack([lambda_q1[l], lambda_k1[l], lambda_q2[l], lambda_k2[l]]).astype(F32)

        def tail(h, o_a, o_b, tm_mix, tm_moe):
            h1, h1b, gates = _mix(h, o_a, o_b, w_out_b, l, b_out[l], ln_mix_g[l], ln_mix_b[l],
                                  wr_hi, wr_lo, rb, tm_mix)
            return _moe(h1, h1b, gates, w_up_b, b_up4, w_down_b, b_down_pad, l,
                        ln_ffn_g[l], ln_ffn_b[l], tm_moe)

        o_a, qt1, qt2, ka1, ka2, vt, k_all, v_all = _proj_prompt(
            hp, w_in_b, gln_g, gln_b, gmlp_w_s, b_s_t, kbias, k_all, v_all, l, ATTN_TK)
        o_b = _attn_prompt(qt1, qt2, ka1, ka2, vt, diag, lamp,
                           subln_g[l].reshape(DIFF_V_DIM, 1), lam_init, ATTN_TQ, ATTN_TK)
        hp = tail(hp, o_a, o_b, 1024, 1024)

        o_a, q, kf, vf, kb, vb, vn = _proj_sample(hs, w_in_b, gln_g, gln_b, gmlp_w_s, b_s_t,
                                                  l, ts, tdec)
        o_b = _attn_sample(q, kb, vb, ck_rows, cv_rows, l, lamp,
                           subln_g[l].reshape(1, DIFF_V_DIM), lam_init, nbs, tdec)
        hs = tail(hs, o_a, o_b, ts, ts)
        ksm.append(kf)
        vsm.append(vf)
        gvs.append(vn)

    hd = (N_DIFF_HEADS, DIFF_V_DIM)
    return (hp.reshape(1, seq, d),
            hs.reshape(nbs, tdec, d),
            k_all.reshape(DEPTH, 1, seq, *hd),
            v_all.reshape(DEPTH, 1, seq, *hd),
            jnp.stack(ksm).reshape(DEPTH, nbs, tdec, *hd),
            jnp.stack(vsm).reshape(DEPTH, nbs, tdec, *hd),
            jnp.stack(gvs).reshape(DEPTH, nbs, tdec, D_GMLP))
```
